```python
import jax, jax.numpy as jnp
from jax import lax
import numpy as np

D_MODEL = 2048
BATCH = 2
SEQ = 4096
DEPTH = 4
DEC_BATCH = 32
DEC_SEQ = 4
PAST_LEN = 16384
PAGE_SIZE = 128

RET_DK = 128
RET_DV = 128
RET_HEADS = (D_MODEL // 2) // RET_DV
RET_CHUNK = 128
ATT_HEAD_DIM = 64
ATT_Q_HEADS = (D_MODEL // 4) // ATT_HEAD_DIM
ATT_KV_HEADS = 2
ATT_GROUP = ATT_Q_HEADS // ATT_KV_HEADS
WINDOW = 128
CONV_WIDTH = 3
CONV_CH = D_MODEL - RET_HEADS * RET_DV - ATT_Q_HEADS * ATT_HEAD_DIM
D_FF = -(-8 * D_MODEL // (3 * 256)) * 256
N_MOD = 6
EPS = 1e-6
NEG_INF = -1e30
SPLIT_SIZES = (RET_HEADS * RET_DK, RET_HEADS * RET_DK, RET_HEADS * RET_DV, RET_HEADS * RET_DV,
               ATT_Q_HEADS * ATT_HEAD_DIM, ATT_KV_HEADS * ATT_HEAD_DIM, ATT_KV_HEADS * ATT_HEAD_DIM,
               CONV_CH, CONV_CH, CONV_CH)
PROJ_WIDTH = sum(SPLIT_SIZES)

kernel_name = 'hybrid_retention_swa_shortconv_step'


def _split_points():
    pts, acc = [], 0
    for s in SPLIT_SIZES[:-1]:
        acc += s
        pts.append(acc)
    return pts


def _rmsnorm(x, g):
    xf = x.astype(jnp.float32)
    y = xf * lax.rsqrt(jnp.mean(xf * xf, axis=-1, keepdims=True) + EPS)
    return (y * g.astype(jnp.float32)).astype(x.dtype)


def _ret_log_gamma():
    return jnp.log1p(-jnp.exp2(-5.0 - jnp.arange(RET_HEADS, dtype=jnp.float32)))


def _alibi_slopes():
    h = jnp.arange(1, ATT_Q_HEADS + 1, dtype=jnp.float32)
    return jnp.exp2(-8.0 * h / ATT_Q_HEADS).reshape(ATT_KV_HEADS, ATT_GROUP)


def _retention_chunk(q, k, v, s0, log_gamma):
    t = q.shape[1]
    pos = jnp.arange(t, dtype=jnp.float32)
    diff = pos[:, None] - pos[None, :]
    decay = jnp.where(diff[None] >= 0,
                      jnp.exp(jnp.maximum(diff, 0.0)[None] * log_gamma[:, None, None]), 0.0)
    scores = jnp.einsum('bihd,bjhd->bhij', q, k) * decay[None]
    intra = jnp.einsum('bhij,bjhe->bihe', scores, v)
    q_decay = jnp.exp((pos[:, None] + 1.0) * log_gamma[None, :])
    inter = jnp.einsum('bihd,bhde->bihe', q, s0) * q_decay[None, :, :, None]
    k_decay = jnp.exp((t - 1.0 - pos)[:, None] * log_gamma[None, :])
    s_new = (jnp.exp(t * log_gamma)[None, :, None, None] * s0
             + jnp.einsum('bjhd,bjhe->bhde', k * k_decay[None, :, :, None], v))
    return intra + inter, s_new


def _retention_prompt(q, k, v, log_gamma):
    b, s = q.shape[:2]
    nc = s // RET_CHUNK

    def to_chunks(a):
        return a.reshape((b, nc, RET_CHUNK) + a.shape[2:]).swapaxes(0, 1)

    def step(state, qkv):
        qc, kc, vc = qkv
        y, state = _retention_chunk(qc, kc, vc, state, log_gamma)
        return state, y

    s0 = jnp.zeros((b, RET_HEADS, RET_DK, RET_DV), jnp.float32)
    s_fin, ys = lax.scan(step, s0, (to_chunks(q), to_chunks(k), to_chunks(v)))
    return ys.swapaxes(0, 1).reshape(b, s, RET_HEADS, RET_DV), s_fin


def _sink_attention(q, k, v, q_pos, k_pos, sinks, slopes):
    s = jnp.einsum('bnqhgd,bnkhd->bnhgqk', q, k) * (ATT_HEAD_DIM ** -0.5)
    dist = q_pos[:, :, None] - k_pos[:, None, :]
    allowed = (dist >= 0) & (dist < WINDOW) & (k_pos[:, None, :] >= 0)
    s = s - slopes[None, None, :, :, None, None] * dist.astype(jnp.float32)[None, :, None, None]
    s = jnp.where(allowed[None, :, None, None], s, NEG_INF)
    sink = jnp.broadcast_to(sinks.reshape(ATT_KV_HEADS, ATT_GROUP)[None, None, :, :, None, None],
                            s.shape[:-1] + (1,))
    p = jax.nn.softmax(jnp.concatenate([s, sink], axis=-1), axis=-1)[..., :-1]
    return jnp.einsum('bnhgqk,bnkhd->bnqhgd', p, v)


def _swa_prompt(q, k, v, sinks, slopes):
    b, s = q.shape[:2]
    nb = s // WINDOW
    qb = q.reshape(b, nb, WINDOW, ATT_KV_HEADS, ATT_GROUP, ATT_HEAD_DIM)

    def band(a):
        ap = jnp.pad(a, ((0, 0), (WINDOW, 0), (0, 0), (0, 0)))
        ap = ap.reshape(b, nb + 1, WINDOW, ATT_KV_HEADS, ATT_HEAD_DIM)
        return jnp.concatenate([ap[:, :-1], ap[:, 1:]], axis=2)

    q_pos = jnp.arange(s, dtype=jnp.int32).reshape(nb, WINDOW)
    kp = jnp.arange(-WINDOW, s, dtype=jnp.int32).reshape(nb + 1, WINDOW)
    k_pos = jnp.concatenate([kp[:-1], kp[1:]], axis=1)
    o = _sink_attention(qb, band(k), band(v), q_pos, k_pos, sinks, slopes)
    return o.reshape(b, s, ATT_Q_HEADS * ATT_HEAD_DIM)


def _swa_step(q, k, v, buf_k, buf_v, sinks, slopes):
    b, t = q.shape[:2]
    w = buf_k.shape[1]
    k_all = jnp.concatenate([buf_k, k], axis=1)
    v_all = jnp.concatenate([buf_v, v], axis=1)
    q_pos = (PAST_LEN + jnp.arange(t, dtype=jnp.int32))[None]
    k_pos = jnp.concatenate([PAST_LEN - w + jnp.arange(w, dtype=jnp.int32),
                             PAST_LEN + jnp.arange(t, dtype=jnp.int32)])[None]
    o = _sink_attention(q[:, None], k_all[:, None], v_all[:, None], q_pos, k_pos, sinks, slopes)
    return o.reshape(b, t, ATT_Q_HEADS * ATT_HEAD_DIM), k_all[:, -w:], v_all[:, -w:]


def _short_conv(gate_b, gate_c, hx, buf, conv_w):
    u = gate_c * hx
    up = jnp.concatenate([buf, u], axis=1)
    t = u.shape[1]
    y = conv_w[0] * up[:, 0:t]
    for j in range(1, CONV_WIDTH):
        y = y + conv_w[j] * up[:, j:j + t]
    return gate_b * y, up[:, -(CONV_WIDTH - 1):]


def _mixers(h, w_in_l, w_out_l, conv_w_l, sinks_l, s_ret, buf_k, buf_v, buf_conv, w_buf, sdt, is_prompt):
    f32 = jnp.float32
    b, t, _ = h.shape
    proj = (h @ w_in_l).astype(f32)
    rq, rk, rv, rg, aq, ak, av, cb, cc, ch = jnp.split(proj, _split_points(), axis=-1)
    rq = rq.reshape(b, t, RET_HEADS, RET_DK)
    rk = rk.reshape(b, t, RET_HEADS, RET_DK) * (RET_DK ** -0.5)
    rv = rv.reshape(b, t, RET_HEADS, RET_DV)
    log_gamma = _ret_log_gamma()
    if is_prompt:
        ry, s_new = _retention_prompt(rq, rk, rv, log_gamma)
    else:
        ry, s_new = _retention_chunk(rq, rk, rv, s_ret.astype(f32), log_gamma)
    ry = ry * lax.rsqrt(jnp.mean(ry * ry, axis=-1, keepdims=True) + EPS)
    ret_out = jax.nn.silu(rg) * ry.reshape(b, t, RET_HEADS * RET_DV)
    aq = aq.reshape(b, t, ATT_KV_HEADS, ATT_GROUP, ATT_HEAD_DIM)
    ak = ak.reshape(b, t, ATT_KV_HEADS, ATT_HEAD_DIM)
    av = av.reshape(b, t, ATT_KV_HEADS, ATT_HEAD_DIM)
    sinks = sinks_l.astype(f32)
    slopes = _alibi_slopes()
    if is_prompt:
        att_out = _swa_prompt(aq, ak, av, sinks, slopes)
        k_new, v_new = ak[:, -w_buf:], av[:, -w_buf:]
    else:
        att_out, k_new, v_new = _swa_step(aq, ak, av, buf_k.astype(f32), buf_v.astype(f32), sinks, slopes)
    conv_out, conv_new = _short_conv(cb, cc, ch, buf_conv.astype(f32), conv_w_l.astype(f32))
    mixed = jnp.concatenate([ret_out, att_out, conv_out], axis=-1).astype(h.dtype)
    out = mixed @ w_out_l
    return out, (s_new.astype(sdt), k_new.astype(sdt), v_new.astype(sdt), conv_new.astype(sdt))


def _layer(x, c, mix, norm_g_l, w_ada_l, b_ada_l, w_gate, w_up, w_down):
    mod = jax.nn.silu(c) @ w_ada_l + b_ada_l
    sh1, sc1, g1, sh2, sc2, g2 = jnp.split(mod[:, None, :], N_MOD, axis=-1)
    h = _rmsnorm(x, norm_g_l[0]) * (1.0 + sc1) + sh1
    o, new_state = mix(h)
    x = x + g1 * _rmsnorm(o, norm_g_l[1])
    h = _rmsnorm(x, norm_g_l[2]) * (1.0 + sc2) + sh2
    f = (jax.nn.silu(h @ w_gate) * (h @ w_up)) @ w_down
    x = x + g2 * _rmsnorm(f, norm_g_l[3])
    return x, new_state


def setup_inputs(seed: int = 0) -> dict:
    key = jax.random.key(seed)
    ks = jax.random.split(key, 20)
    f32 = jnp.float32
    w_buf = min(WINDOW, PAST_LEN)

    def nrm(k, shape, scale):
        return jax.random.normal(k, shape, f32) * scale

    return {
        'x_prompt': nrm(ks[0], (BATCH, SEQ, D_MODEL), 1.0),
        'x_sample': nrm(ks[1], (DEC_BATCH, DEC_SEQ, D_MODEL), 1.0),
        'state_ret': nrm(ks[2], (DEPTH, DEC_BATCH, RET_HEADS, RET_DK, RET_DV), 1.0),
        'cache_win_k': nrm(ks[3], (DEPTH, DEC_BATCH, w_buf, ATT_KV_HEADS, ATT_HEAD_DIM), 1.0),
        'cache_win_v': nrm(ks[4], (DEPTH, DEC_BATCH, w_buf, ATT_KV_HEADS, ATT_HEAD_DIM), 1.0),
        'state_conv': nrm(ks[5], (DEPTH, DEC_BATCH, CONV_WIDTH - 1, CONV_CH), 1.0),
        'c_prompt': nrm(ks[6], (BATCH, D_MODEL), 1.0),
        'c_sample': nrm(ks[7], (DEC_BATCH, D_MODEL), 1.0),
        'w_in': nrm(ks[8], (DEPTH, D_MODEL, PROJ_WIDTH), D_MODEL ** -0.5),
        'w_out': nrm(ks[9], (DEPTH, D_MODEL, D_MODEL), D_MODEL ** -0.5),
        'conv_w': nrm(ks[10], (DEPTH, CONV_WIDTH, CONV_CH), CONV_WIDTH ** -0.5),
        'attn_sinks': nrm(ks[11], (DEPTH, ATT_Q_HEADS), 1.0),
        'norm_g': 1.0 + nrm(ks[12], (DEPTH, 4, D_MODEL), 0.05),
        'w_ada': nrm(ks[13], (DEPTH, D_MODEL, N_MOD * D_MODEL), 0.5 * D_MODEL ** -0.5),
        'b_ada': nrm(ks[14], (DEPTH, N_MOD * D_MODEL), 0.01),
        'w_ff_gate': nrm(ks[15], (DEPTH, D_MODEL, D_FF), D_MODEL ** -0.5),
        'w_ff_up': nrm(ks[16], (DEPTH, D_MODEL, D_FF), D_MODEL ** -0.5),
        'w_ff_down': nrm(ks[17], (DEPTH, D_FF, D_MODEL), D_FF ** -0.5),
    }


def reference(x_prompt, x_sample, state_ret, cache_win_k, cache_win_v, state_conv, c_prompt, c_sample,
              w_in, w_out, conv_w, attn_sinks, norm_g, w_ada, b_ada, w_ff_gate, w_ff_up, w_ff_down):
    w_buf = cache_win_k.shape[2]
    sdt = state_ret.dtype
    zeros_conv = jnp.zeros((x_prompt.shape[0], CONV_WIDTH - 1, CONV_CH), x_prompt.dtype)
    y_p, y_s = x_prompt, x_sample
    ret_p, ret_s, wk_p, wk_s, wv_p, wv_s, cv_p, cv_s = [], [], [], [], [], [], [], []
    for l in range(DEPTH):
        ffn = (norm_g[l], w_ada[l], b_ada[l], w_ff_gate[l], w_ff_up[l], w_ff_down[l])
        mix_p = lambda h: _mixers(h, w_in[l], w_out[l], conv_w[l], attn_sinks[l], None, None, None,
                                  zeros_conv, w_buf, sdt, True)
        mix_s = lambda h: _mixers(h, w_in[l], w_out[l], conv_w[l], attn_sinks[l], state_ret[l],
                                  cache_win_k[l], cache_win_v[l], state_conv[l], w_buf, sdt, False)
        y_p, (sr_p, k_p, v_p, c_p) = _layer(y_p, c_prompt, mix_p, *ffn)
        y_s, (sr_s, k_s, v_s, c_s) = _layer(y_s, c_sample, mix_s, *ffn)
        ret_p.append(sr_p); ret_s.append(sr_s)
        wk_p.append(k_p); wk_s.append(k_s)
        wv_p.append(v_p); wv_s.append(v_s)
        cv_p.append(c_p); cv_s.append(c_s)
    return (y_p, y_s, jnp.stack(ret_p), jnp.stack(ret_s), jnp.stack(wk_p), jnp.stack(wk_s),
            jnp.stack(wv_p), jnp.stack(wv_s), jnp.stack(cv_p), jnp.stack(cv_s))
```

```python
import functools
import math

import numpy as np
import jax
import jax.numpy as jnp
from jax import lax
from jax.experimental import pallas as pl
from jax.experimental.pallas import tpu as pltpu

F32 = jnp.float32
BF16 = jnp.bfloat16

D_MODEL = 2048
DEPTH = 4
PAST_LEN = 16384
RET_DK = 128
RET_DV = 128
RET_HEADS = 8
CHUNK = 128
ATT_HEAD_DIM = 64
ATT_Q_HEADS = 8
ATT_KV_HEADS = 2
ATT_GROUP = 4
WINDOW = 128
CONV_WIDTH = 3
CONV_CH = 512
D_FF = 5632
N_MOD = 6
EPS = 1e-6
NEG_INF = -1e30
PROJ_WIDTH = 6400
OFF_RQ, OFF_RK, OFF_RV, OFF_RG = 0, 1024, 2048, 3072
OFF_AQ, OFF_AK, OFF_AV = 4096, 4608, 4736
OFF_CB, OFF_CC, OFF_CH = 4864, 5376, 5888
MOD_SH1, MOD_SC1, MOD_G1, MOD_SH2, MOD_SC2, MOD_G2 = range(6)

SAMPLE_PAD_T = 16
MOD_ROWS = 48
VMEM_LIMIT_V7X = 56 * 1024 * 1024

LOG_GAMMA = [math.log1p(-(2.0 ** (-5.0 - h))) for h in range(RET_HEADS)]
ALIBI_SLOPES = [2.0 ** (-8.0 * (h + 1) / ATT_Q_HEADS) for h in range(ATT_Q_HEADS)]


def _silu(a):
    return a / (1.0 + jnp.exp(-a))


def _rms(x, gain):
    return x * lax.rsqrt(jnp.mean(x * x, axis=-1, keepdims=True) + EPS) * gain


def _params(*sem):
    return pltpu.CompilerParams(dimension_semantics=sem, vmem_limit_bytes=VMEM_LIMIT_V7X)


def _ada_kernel(c_ref, w_ref, b_ref, o_ref):
    a = _silu(c_ref[...]).astype(BF16)
    o_ref[0] = jnp.dot(a, w_ref[0].astype(BF16), preferred_element_type=F32) + b_ref[0]


def _ada_call(c_all, w_ada, b_ada):
    tn = 1024
    n = N_MOD * D_MODEL
    return pl.pallas_call(
        _ada_kernel,
        grid=(DEPTH, n // tn),
        in_specs=[
            pl.BlockSpec((MOD_ROWS, D_MODEL), lambda l, j: (0, 0)),
            pl.BlockSpec((1, D_MODEL, tn), lambda l, j: (l, 0, j)),
            pl.BlockSpec((1, 1, tn), lambda l, j: (l, 0, j)),
        ],
        out_specs=pl.BlockSpec((1, MOD_ROWS, tn), lambda l, j: (l, 0, j)),
        out_shape=jax.ShapeDtypeStruct((DEPTH, MOD_ROWS, n), F32),
        compiler_params=_params("arbitrary", "arbitrary"),
        name="ada_mod",
    )(c_all, w_ada, b_ada.reshape(DEPTH, 1, n))


class _Tiling:
    def __init__(self, n_tiles, g, r, mod_row0, tiles_per_mod_row):
        self.n_tiles, self.g, self.r = n_tiles, g, r
        self.mod_row0, self.tiles_per_mod_row = mod_row0, tiles_per_mod_row
        self.rows = g * r

    def x_spec(self):
        return pl.BlockSpec((self.g, self.r, D_MODEL), lambda i, *_: (i, 0, 0))

    def rows_spec(self, width):
        return pl.BlockSpec((self.rows, width), lambda i, *_: (i, 0))

    def mod_spec(self, layer, chunk):
        row0, per = self.mod_row0, self.tiles_per_mod_row
        if self.g == 1:
            return pl.BlockSpec((1, 1, 1, D_MODEL), lambda i, *_: (layer, row0 + i // per, 0, chunk))
        return pl.BlockSpec((1, self.g, 1, D_MODEL), lambda i, *_: (layer, 0, 0, chunk))


def _gain_spec(layer, which):
    return pl.BlockSpec((1, 1, D_MODEL), lambda i, *_: (layer * 4 + which, 0, 0))


def _inproj_kernel(x_ref, gain_ref, sc_ref, sh_ref, w_ref, o_ref, h_scr):
    @pl.when(pl.program_id(1) == 0)
    def _():
        h = _rms(x_ref[...], gain_ref[...]) * (1.0 + sc_ref[0]) + sh_ref[0]
        h_scr[...] = h.reshape(h_scr.shape).astype(BF16)

    o_ref[...] = jnp.dot(h_scr[...], w_ref[0], preferred_element_type=F32).astype(o_ref.dtype)


def _inproj_call(til, layer, x, gains, mod4, w_in_b, out_dtype):
    tn = 1280
    return pl.pallas_call(
        _inproj_kernel,
        grid=(til.n_tiles, PROJ_WIDTH // tn),
        in_specs=[
            til.x_spec(),
            _gain_spec(layer, 0),
            til.mod_spec(layer, MOD_SC1),
            til.mod_spec(layer, MOD_SH1),
            pl.BlockSpec((1, D_MODEL, tn), lambda i, j: (layer, 0, j)),
        ],
        out_specs=pl.BlockSpec((til.rows, tn), lambda i, j: (i, j)),
        out_shape=jax.ShapeDtypeStruct((til.n_tiles * til.rows, PROJ_WIDTH), out_dtype),
        scratch_shapes=[pltpu.VMEM((til.rows, D_MODEL), BF16)],
        compiler_params=_params("arbitrary", "arbitrary"),
        name="in_proj",
    )(x, gains, mod4, mod4, w_in_b)


def _outproj_kernel(m_ref, w_ref, x_ref, gain1_ref, gain2_ref, g1_ref, sc_ref, sh_ref, x1_ref, h2_ref):
    o = jnp.dot(m_ref[...], w_ref[0], preferred_element_type=F32).reshape(x_ref.shape)
    x1 = x_ref[...] + g1_ref[0] * _rms(o, gain1_ref[...])
    x1_ref[...] = x1
    h2 = _rms(x1, gain2_ref[...]) * (1.0 + sc_ref[0]) + sh_ref[0]
    h2_ref[...] = h2.reshape(h2_ref.shape).astype(BF16)


def _outproj_call(til, layer, mixed, x, gains, mod4, w_out_b):
    return pl.pallas_call(
        _outproj_kernel,
        grid=(til.n_tiles,),
        in_specs=[
            til.rows_spec(D_MODEL),
            pl.BlockSpec((1, D_MODEL, D_MODEL), lambda i: (layer, 0, 0), pipeline_mode=pl.Buffered(1)),
            til.x_spec(),
            _gain_spec(layer, 1),
            _gain_spec(layer, 2),
            til.mod_spec(layer, MOD_G1),
            til.mod_spec(layer, MOD_SC2),
            til.mod_spec(layer, MOD_SH2),
        ],
        out_specs=[til.x_spec(), til.rows_spec(D_MODEL)],
        out_shape=[
            jax.ShapeDtypeStruct(x.shape, F32),
            jax.ShapeDtypeStruct((til.n_tiles * til.rows, D_MODEL), BF16),
        ],
        compiler_params=_params("arbitrary"),
        name="out_proj",
    )(mixed, w_out_b, x, gains, gains, mod4, mod4, mod4)


def _ffn_kernel(h_ref, x_ref, wg_ref, wu_ref, wd_ref, gain_ref, g2_ref, o_ref, acc_ref):
    j = pl.program_id(1)
    h = h_ref[...]
    a = jnp.dot(h, wg_ref[0], preferred_element_type=F32)
    b = jnp.dot(h, wu_ref[0], preferred_element_type=F32)
    hid = (_silu(a) * b).astype(BF16)
    part = jnp.dot(hid, wd_ref[0], preferred_element_type=F32)

    @pl.when(j == 0)
    def _():
        acc_ref[...] = part

    @pl.when(j > 0)
    def _():
        acc_ref[...] += part

    @pl.when(j == pl.num_programs(1) - 1)
    def _():
        f = acc_ref[...].reshape(x_ref.shape)
        o_ref[...] = x_ref[...] + g2_ref[0] * _rms(f, gain_ref[...])


def _ffn_call(til, layer, h2, x1, gains, mod4, wg_b, wu_b, wd_b):
    tf = 512
    return pl.pallas_call(
        _ffn_kernel,
        grid=(til.n_tiles, D_FF // tf),
        in_specs=[
            til.rows_spec(D_MODEL),
            til.x_spec(),
            pl.BlockSpec((1, D_MODEL, tf), lambda i, j: (layer, 0, j)),
            pl.BlockSpec((1, D_MODEL, tf), lambda i, j: (layer, 0, j)),
            pl.BlockSpec((1, tf, D_MODEL), lambda i, j: (layer, j, 0)),
            _gain_spec(layer, 3),
            til.mod_spec(layer, MOD_G2),
        ],
        out_specs=til.x_spec(),
        out_shape=jax.ShapeDtypeStruct(x1.shape, F32),
        scratch_shapes=[pltpu.VMEM((til.rows, D_MODEL), F32)],
        compiler_params=_params("arbitrary", "arbitrary"),
        name="ffn",
    )(h2, x1, wg_b, wu_b, wd_b, gains, mod4)


def _mixer_tables(t_rows, t_valid):
    scale = RET_DK ** -0.5
    i = np.arange(t_rows)[:, None].astype(np.float64)
    j = np.arange(CHUNK)[None, :].astype(np.float64)
    jr = np.arange(CHUNK)[:, None].astype(np.float64)
    dtab = np.zeros((RET_HEADS, t_rows, CHUNK), np.float64)
    qtab = np.zeros((RET_HEADS, t_rows, CHUNK), np.float64)
    ktab = np.zeros((RET_HEADS, CHUNK, CHUNK), np.float64)
    sdec = []
    for h, lg in enumerate(LOG_GAMMA):
        causal = (i >= j) & (i < t_valid) & (j < t_valid)
        dtab[h] = np.where(causal, scale * np.exp(np.maximum(i - j, 0.0) * lg), 0.0)
        qtab[h] = np.where(i < t_valid, np.exp((i + 1.0) * lg), 0.0) * np.ones((1, CHUNK))
        ktab[h] = np.where(jr < t_valid, scale * np.exp(np.maximum(t_valid - 1.0 - jr, 0.0) * lg), 0.0) * np.ones((1, CHUNK))
        sdec.append(float(np.exp(t_valid * lg)))
    return (jnp.asarray(dtab, F32), jnp.asarray(qtab, F32), jnp.asarray(ktab, F32)), sdec


def _mixer_math(proj, state, k_prev, v_prev, u_prev, tabs, sdec, conv_w, sinks, min_key, mixed_ref, ubuf):
    dtab_ref, qtab_ref, ktab_ref = tabs
    t_rows = mixed_ref.shape[0]
    pad = CHUNK - t_rows

    def pad_rows(a):
        if pad == 0:
            return a
        return jnp.concatenate([a, jnp.zeros((pad, a.shape[1]), a.dtype)], axis=0)

    new_state = []
    for h in range(RET_HEADS):
        q = proj(OFF_RQ + h * RET_DK, OFF_RQ + (h + 1) * RET_DK).astype(BF16)
        k = pad_rows(proj(OFF_RK + h * RET_DK, OFF_RK + (h + 1) * RET_DK).astype(BF16))
        v = pad_rows(proj(OFF_RV + h * RET_DV, OFF_RV + (h + 1) * RET_DV).astype(BF16))
        gate = proj(OFF_RG + h * RET_DV, OFF_RG + (h + 1) * RET_DV).astype(F32)
        s = lax.dot_general(q, k, (((1,), (1,)), ((), ())), preferred_element_type=F32)
        p = (s * dtab_ref[h]).astype(BF16)
        intra = jnp.dot(p, v, preferred_element_type=F32)
        s0 = state[h]
        inter = jnp.dot(q, s0.astype(BF16), preferred_element_type=F32) * qtab_ref[h]
        ry = intra + inter
        ks = (k.astype(F32) * ktab_ref[h]).astype(BF16)
        kv = lax.dot_general(ks, v, (((0,), (0,)), ((), ())), preferred_element_type=F32)
        new_state.append(sdec[h] * s0 + kv)
        ryn = ry * lax.rsqrt(jnp.mean(ry * ry, axis=-1, keepdims=True) + EPS)
        mixed_ref[:, h * RET_DV:(h + 1) * RET_DV] = (_silu(gate) * ryn).astype(mixed_ref.dtype)

    k_cur = pad_rows(proj(OFF_AK, OFF_AK + 128).astype(BF16))
    v_cur = pad_rows(proj(OFF_AV, OFF_AV + 128).astype(BF16))
    k_all = jnp.concatenate([k_prev, k_cur], axis=0)
    v_all = jnp.concatenate([v_prev, v_cur], axis=0)
    qi = lax.broadcasted_iota(jnp.int32, (t_rows, 2 * CHUNK), 0)
    kj = lax.broadcasted_iota(jnp.int32, (t_rows, 2 * CHUNK), 1)
    dist = qi + CHUNK - kj
    allowed = (dist >= 0) & (dist < WINDOW)
    if min_key is not None:
        allowed = allowed & (kj >= min_key)
    dist_f = dist.astype(F32)
    att_base = RET_HEADS * RET_DV
    for hh in range(ATT_Q_HEADS):
        kvh = hh // ATT_GROUP
        lo, hi = kvh * ATT_HEAD_DIM, (kvh + 1) * ATT_HEAD_DIM
        q = proj(OFF_AQ + hh * ATT_HEAD_DIM, OFF_AQ + (hh + 1) * ATT_HEAD_DIM).astype(BF16)
        s = lax.dot_general(q, k_all[:, lo:hi], (((1,), (1,)), ((), ())), preferred_element_type=F32)
        s = s * (ATT_HEAD_DIM ** -0.5) - ALIBI_SLOPES[hh] * dist_f
        s = jnp.where(allowed, s, NEG_INF)
        sink = sinks[hh]
        m = jnp.maximum(jnp.max(s, axis=-1, keepdims=True), sink)
        e = jnp.exp(s - m)
        den = jnp.sum(e, axis=-1, keepdims=True) + jnp.exp(sink - m)
        p = (e / den).astype(BF16)
        o = jnp.dot(p, v_all[:, lo:hi], preferred_element_type=F32)
        c0 = att_base + hh * ATT_HEAD_DIM
        mixed_ref[:, c0:c0 + ATT_HEAD_DIM] = o.astype(mixed_ref.dtype)

    gate_b = proj(OFF_CB, OFF_CB + CONV_CH).astype(F32)
    u = proj(OFF_CC, OFF_CC + CONV_CH).astype(F32) * proj(OFF_CH, OFF_CH + CONV_CH).astype(F32)
    ubuf[6:8, :] = u_prev
    ubuf[8:8 + t_rows, :] = u
    y = conv_w[0:1, :] * ubuf[6:6 + t_rows, :] + conv_w[1:2, :] * ubuf[7:7 + t_rows, :] + conv_w[2:3, :] * u
    c0 = att_base + ATT_Q_HEADS * ATT_HEAD_DIM
    mixed_ref[:, c0:c0 + CONV_CH] = (gate_b * y).astype(mixed_ref.dtype)
    return new_state, k_cur, v_cur


def _mix_prompt_kernel(sink_ref, proj_ref, dtab_ref, qtab_ref, ktab_ref, convw_ref,
                       mixed_ref, sret_ref, knew_ref, vnew_ref, cnew_ref,
                       s_scr, kprev_scr, vprev_scr, uprev_scr, ubuf, *, layer, sdec):
    c = pl.program_id(1)

    @pl.when(c == 0)
    def _():
        s_scr[...] = jnp.zeros_like(s_scr)
        kprev_scr[...] = jnp.zeros_like(kprev_scr)
        vprev_scr[...] = jnp.zeros_like(vprev_scr)
        uprev_scr[...] = jnp.zeros_like(uprev_scr)

    proj = lambda lo, hi: proj_ref[0, :, lo:hi]
    sinks = [sink_ref[layer, hh] for hh in range(ATT_Q_HEADS)]
    state = [s_scr[h] for h in range(RET_HEADS)]
    new_state, k_cur, v_cur = _mixer_math(
        proj, state, kprev_scr[...], vprev_scr[...], uprev_scr[...], (dtab_ref, qtab_ref, ktab_ref), sdec,
        convw_ref[0], sinks, jnp.where(c == 0, CHUNK, 0), mixed_ref.at[0], ubuf)
    for h in range(RET_HEADS):
        s_scr[h] = new_state[h]
    kprev_scr[...] = k_cur
    vprev_scr[...] = v_cur
    uprev_scr[...] = ubuf[CHUNK + 6:CHUNK + 8, :]

    @pl.when(c == pl.num_programs(1) - 1)
    def _():
        for h in range(RET_HEADS):
            sret_ref[0, h] = new_state[h]
        knew_ref[0] = k_cur.astype(F32)
        vnew_ref[0] = v_cur.astype(F32)
        cnew_ref[0] = ubuf[CHUNK + 6:CHUNK + 8, :]


def _mix_prompt_call(layer, proj, conv_w, attn_sinks, tables, sdec):
    b, s = proj.shape[0], proj.shape[1]
    const3 = lambda bi, ci: (0, 0, 0)
    return pl.pallas_call(
        functools.partial(_mix_prompt_kernel, layer=layer, sdec=sdec),
        grid=(b, s // CHUNK),
        in_specs=[
            pl.BlockSpec(memory_space=pltpu.SMEM),
            pl.BlockSpec((1, CHUNK, PROJ_WIDTH), lambda bi, ci: (bi, ci, 0)),
            pl.BlockSpec((RET_HEADS, CHUNK, CHUNK), const3),
            pl.BlockSpec((RET_HEADS, CHUNK, CHUNK), const3),
            pl.BlockSpec((RET_HEADS, CHUNK, CHUNK), const3),
            pl.BlockSpec((1, CONV_WIDTH, CONV_CH), lambda bi, ci: (layer, 0, 0)),
        ],
        out_specs=[
            pl.BlockSpec((1, CHUNK, D_MODEL), lambda bi, ci: (bi, ci, 0)),
            pl.BlockSpec((1, RET_HEADS, RET_DK, RET_DV), lambda bi, ci: (bi, 0, 0, 0)),
            pl.BlockSpec((1, WINDOW, 128), lambda bi, ci: (bi, 0, 0)),
            pl.BlockSpec((1, WINDOW, 128), lambda bi, ci: (bi, 0, 0)),
            pl.BlockSpec((1, CONV_WIDTH - 1, CONV_CH), lambda bi, ci: (bi, 0, 0)),
        ],
        out_shape=[
            jax.ShapeDtypeStruct((b, s, D_MODEL), BF16),
            jax.ShapeDtypeStruct((b, RET_HEADS, RET_DK, RET_DV), F32),
            jax.ShapeDtypeStruct((b, WINDOW, 128), F32),
            jax.ShapeDtypeStruct((b, WINDOW, 128), F32),
            jax.ShapeDtypeStruct((b, CONV_WIDTH - 1, CONV_CH), F32),
        ],
        scratch_shapes=[
            pltpu.VMEM((RET_HEADS, RET_DK, RET_DV), F32),
            pltpu.VMEM((CHUNK, 128), BF16),
            pltpu.VMEM((CHUNK, 128), BF16),
            pltpu.VMEM((CONV_WIDTH - 1, CONV_CH), F32),
            pltpu.VMEM((CHUNK + 8, CONV_CH), F32),
        ],
        compiler_params=_params("arbitrary", "arbitrary"),
        name="mix_prompt",
    )(attn_sinks, proj, *tables, conv_w)


def _mix_sample_kernel(sink_ref, proj_ref, dtab_ref, qtab_ref, ktab_ref, convw_ref,
                       sin_ref, kbuf_ref, vbuf_ref, cbuf_ref,
                       mixed_ref, sret_ref, knew_ref, vnew_ref, cnew_ref,
                       ubuf, kvbuf, *, layer, sdec, t_valid):
    proj = lambda lo, hi: proj_ref[0, :, lo:hi]
    sinks = [sink_ref[layer, hh] for hh in range(ATT_Q_HEADS)]
    state = [sin_ref[0, h] for h in range(RET_HEADS)]
    new_state, k_cur, v_cur = _mixer_math(
        proj, state, kbuf_ref[0].astype(BF16), vbuf_ref[0].astype(BF16), cbuf_ref[0],
        (dtab_ref, qtab_ref, ktab_ref), sdec, convw_ref[0], sinks, None, mixed_ref.at[0], ubuf)
    for h in range(RET_HEADS):
        sret_ref[0, h] = new_state[h]
    t_rows = mixed_ref.shape[1]
    for src_ref, off, dst_ref in ((kbuf_ref, OFF_AK, knew_ref), (vbuf_ref, OFF_AV, vnew_ref)):
        kvbuf[0:WINDOW, :] = src_ref[0]
        kvbuf[WINDOW:WINDOW + t_rows, :] = proj(off, off + 128).astype(F32)
        dst_ref[0] = kvbuf[t_valid:t_valid + WINDOW, :]
    cnew_ref[0] = ubuf[8 + t_valid - 2:8 + t_valid, :]


def _mix_sample_call(layer, proj, conv_w, attn_sinks, tables, sdec, s_in, kbuf, vbuf, cbuf, t_valid):
    nb, t_rows = proj.shape[0], proj.shape[1]
    const3 = lambda bi: (0, 0, 0)
    per_seq3 = lambda bi: (bi, 0, 0)
    per_seq4 = lambda bi: (bi, 0, 0, 0)
    return pl.pallas_call(
        functools.partial(_mix_sample_kernel, layer=layer, sdec=sdec, t_valid=t_valid),
        grid=(nb,),
        in_specs=[
            pl.BlockSpec(memory_space=pltpu.SMEM),
            pl.BlockSpec((1, t_rows, PROJ_WIDTH), per_seq3),
            pl.BlockSpec((RET_HEADS, t_rows, CHUNK), const3),
            pl.BlockSpec((RET_HEADS, t_rows, CHUNK), const3),
            pl.BlockSpec((RET_HEADS, CHUNK, CHUNK), const3),
            pl.BlockSpec((1, CONV_WIDTH, CONV_CH), lambda bi: (layer, 0, 0)),
            pl.BlockSpec((1, RET_HEADS, RET_DK, RET_DV), per_seq4),
            pl.BlockSpec((1, WINDOW, 128), per_seq3),
            pl.BlockSpec((1, WINDOW, 128), per_seq3),
            pl.BlockSpec((1, CONV_WIDTH - 1, CONV_CH), per_seq3),
        ],
        out_specs=[
            pl.BlockSpec((1, t_rows, D_MODEL), per_seq3),
            pl.BlockSpec((1, RET_HEADS, RET_DK, RET_DV), per_seq4),
            pl.BlockSpec((1, WINDOW, 128), per_seq3),
            pl.BlockSpec((1, WINDOW, 128), per_seq3),
            pl.BlockSpec((1, CONV_WIDTH - 1, CONV_CH), per_seq3),
        ],
        out_shape=[
            jax.ShapeDtypeStruct((nb, t_rows, D_MODEL), BF16),
            jax.ShapeDtypeStruct((nb, RET_HEADS, RET_DK, RET_DV), F32),
            jax.ShapeDtypeStruct((nb, WINDOW, 128), F32),
            jax.ShapeDtypeStruct((nb, WINDOW, 128), F32),
            jax.ShapeDtypeStruct((nb, CONV_WIDTH - 1, CONV_CH), F32),
        ],
        scratch_shapes=[
            pltpu.VMEM((t_rows + 8, CONV_CH), F32),
            pltpu.VMEM((WINDOW + t_rows, 128), F32),
        ],
        compiler_params=_params("arbitrary"),
        name="mix_sample",
    )(attn_sinks, proj, *tables, conv_w, s_in, kbuf, vbuf, cbuf)


def kernel(x_prompt, x_sample, state_ret, cache_win_k, cache_win_v, state_conv, c_prompt, c_sample,
           w_in, w_out, conv_w, attn_sinks, norm_g, w_ada, b_ada, w_ff_gate, w_ff_up, w_ff_down):
    batch, seq, _ = x_prompt.shape
    nb, t_valid, _ = x_sample.shape
    w_buf = cache_win_k.shape[2]
    assert w_buf == WINDOW and seq % CHUNK == 0 and t_valid <= SAMPLE_PAD_T and nb + batch <= MOD_ROWS

    w_in_b, w_out_b = w_in.astype(BF16), w_out.astype(BF16)
    wg_b, wu_b, wd_b = w_ff_gate.astype(BF16), w_ff_up.astype(BF16), w_ff_down.astype(BF16)

    c_all = jnp.concatenate([c_sample, c_prompt, jnp.zeros((MOD_ROWS - nb - batch, D_MODEL), F32)], axis=0)
    mod4 = _ada_call(c_all, w_ada, b_ada).reshape(DEPTH, MOD_ROWS, 1, N_MOD * D_MODEL)
    gains = norm_g.reshape(DEPTH * 4, 1, D_MODEL)

    tm = 512
    til_p = _Tiling(batch * seq // tm, 1, tm, nb, seq // tm)
    til_s = _Tiling(1, nb, SAMPLE_PAD_T, 0, 1)
    tabs_p, sdec_p = _mixer_tables(CHUNK, CHUNK)
    tabs_s, sdec_s = _mixer_tables(SAMPLE_PAD_T, t_valid)

    xp = x_prompt.reshape(batch * seq // tm, tm, D_MODEL)
    xs = jnp.pad(x_sample, ((0, 0), (0, SAMPLE_PAD_T - t_valid), (0, 0)))
    kbuf = cache_win_k.reshape(DEPTH, nb, w_buf, 128)
    vbuf = cache_win_v.reshape(DEPTH, nb, w_buf, 128)

    outs_p, outs_s = [], []
    for l in range(DEPTH):
        proj_p = _inproj_call(til_p, l, xp, gains, mod4, w_in_b, BF16).reshape(batch, seq, PROJ_WIDTH)
        proj_s = _inproj_call(til_s, l, xs, gains, mod4, w_in_b, F32).reshape(nb, SAMPLE_PAD_T, PROJ_WIDTH)
        mixed_p, *st_p = _mix_prompt_call(l, proj_p, conv_w, attn_sinks, tabs_p, sdec_p)
        mixed_s, *st_s = _mix_sample_call(l, proj_s, conv_w, attn_sinks, tabs_s, sdec_s,
                                          state_ret[l], kbuf[l], vbuf[l], state_conv[l], t_valid)
        xp1, hp2 = _outproj_call(til_p, l, mixed_p.reshape(batch * seq, D_MODEL), xp, gains, mod4, w_out_b)
        xs1, hs2 = _outproj_call(til_s, l, mixed_s.reshape(nb * SAMPLE_PAD_T, D_MODEL), xs, gains, mod4, w_out_b)
        xp = _ffn_call(til_p, l, hp2, xp1, gains, mod4, wg_b, wu_b, wd_b)
        xs = _ffn_call(til_s, l, hs2, xs1, gains, mod4, wg_b, wu_b, wd_b)
        outs_p.append(st_p)
        outs_s.append(st_s)

    def stack(outs, idx, shape):
        return jnp.stack([o[idx] for o in outs]).reshape(shape)

    kv_shape_p = (DEPTH, batch, w_buf, ATT_KV_HEADS, ATT_HEAD_DIM)
    kv_shape_s = (DEPTH, nb, w_buf, ATT_KV_HEADS, ATT_HEAD_DIM)
    return (
        xp.reshape(batch, seq, D_MODEL),
        xs[:, :t_valid],
        stack(outs_p, 0, (DEPTH, batch, RET_HEADS, RET_DK, RET_DV)),
        stack(outs_s, 0, (DEPTH, nb, RET_HEADS, RET_DK, RET_DV)),
        stack(outs_p, 1, kv_shape_p),
        stack(outs_s, 1, kv_shape_s),
        stack(outs_p, 2, kv_shape_p),
        stack(outs_s, 2, kv_shape_s),
        stack(outs_p, 3, (DEPTH, batch, CONV_WIDTH - 1, CONV_CH)),
        stack(outs_s, 3, (DEPTH, nb, CONV_WIDTH - 1, CONV_CH)),
    )
```

```python
import functools
import math

import numpy as np
import jax
import jax.numpy as jnp
from jax import lax
from jax.experimental import pallas as pl
from jax.experimental.pallas import tpu as pltpu

F32 = jnp.float32
BF16 = jnp.bfloat16

D_MODEL = 2048
DEPTH = 4
RET_DK = 128
RET_DV = 128
RET_HEADS = 8
CHUNK = 128
ATT_HEAD_DIM = 64
ATT_Q_HEADS = 8
ATT_KV_HEADS = 2
ATT_GROUP = 4
WINDOW = 128
CONV_WIDTH = 3
CONV_CH = 512
D_FF = 5632
N_MOD = 6
EPS = 1e-6
NEG_INF = -1e30
PROJ_WIDTH = 6400
OFF_RQ, OFF_RK, OFF_RV, OFF_RG = 0, 1024, 2048, 3072
OFF_AQ, OFF_AK, OFF_AV = 4096, 4608, 4736
OFF_CB, OFF_CC, OFF_CH = 4864, 5376, 5888
MOD_SH1, MOD_SC1, MOD_G1, MOD_SH2, MOD_SC2, MOD_G2 = range(6)

SEQ_PAD = 16
MOD_ROWS = 48
VMEM_LIMIT_V7X = 56 * 1024 * 1024

LOG_GAMMA = [math.log1p(-(2.0 ** (-5.0 - h))) for h in range(RET_HEADS)]
ALIBI_SLOPES = [2.0 ** (-8.0 * (h + 1) / ATT_Q_HEADS) for h in range(ATT_Q_HEADS)]


def _silu(a):
    return a / (1.0 + jnp.exp(-a))


def _rms(x, gain):
    return x * lax.rsqrt(jnp.mean(x * x, axis=-1, keepdims=True) + EPS) * gain


def _norm_mod(x, gain, scale, shift):
    return _rms(x, gain) * (1.0 + scale) + shift


def _params(*sem):
    return pltpu.CompilerParams(dimension_semantics=sem, vmem_limit_bytes=VMEM_LIMIT_V7X)


def _ada_kernel(c_ref, w_ref, b_ref, o_ref):
    a = _silu(c_ref[...]).astype(BF16)
    o_ref[0] = jnp.dot(a, w_ref[0].astype(BF16), preferred_element_type=F32) + b_ref[0]


def _ada_call(c_all, w_ada, b_ada):
    tn = 1024
    n = N_MOD * D_MODEL
    return pl.pallas_call(
        _ada_kernel,
        grid=(DEPTH, n // tn),
        in_specs=[
            pl.BlockSpec((MOD_ROWS, D_MODEL), lambda l, j: (0, 0)),
            pl.BlockSpec((1, D_MODEL, tn), lambda l, j: (l, 0, j)),
            pl.BlockSpec((1, 1, tn), lambda l, j: (l, 0, j)),
        ],
        out_specs=pl.BlockSpec((1, MOD_ROWS, tn), lambda l, j: (l, 0, j)),
        out_shape=jax.ShapeDtypeStruct((DEPTH, MOD_ROWS, n), F32),
        compiler_params=_params("arbitrary", "arbitrary"),
        name="ada_mod",
    )(c_all, w_ada, b_ada.reshape(DEPTH, 1, n))


class _RowTiles:
    def __init__(self, tm, rows_prompt, rows_sample, seq_len, n_seq):
        assert rows_prompt % tm == 0 and rows_sample % tm == 0 and seq_len % tm == 0 and tm % SEQ_PAD == 0
        self.tm = tm
        self.n_prompt = rows_prompt // tm
        self.n_sample = rows_sample // tm
        self.n_tiles = self.n_prompt + self.n_sample
        self.tiles_per_seq = seq_len // tm
        self.seqs_per_tile = tm // SEQ_PAD
        self.prompt_mod_row0 = n_seq

    def all_rows(self, width):
        return pl.BlockSpec((self.tm, width), lambda i: (i, 0))

    def prompt_rows(self, width):
        last = self.n_prompt - 1
        return pl.BlockSpec((self.tm, width), lambda i: (jnp.minimum(i, last), 0))

    def sample_rows(self, width):
        first = self.n_prompt
        return pl.BlockSpec((self.tm, width), lambda i: (jnp.maximum(i - first, 0), 0))

    def prompt_mod(self, layer, chunk):
        last, per, row0 = self.n_prompt - 1, self.tiles_per_seq, self.prompt_mod_row0
        return pl.BlockSpec((1, 1, 1, D_MODEL), lambda i: (layer, row0 + jnp.minimum(i, last) // per, 0, chunk))

    def sample_mod(self, layer, chunk):
        first = self.n_prompt
        return pl.BlockSpec((1, self.seqs_per_tile, 1, D_MODEL),
                            lambda i: (layer, jnp.maximum(i - first, 0), 0, chunk))


def _gain_spec(layer, which):
    return pl.BlockSpec((1, 1, D_MODEL), lambda i: (layer * 4 + which, 0, 0))


def _const_weight_spec(layer, k, n):
    return pl.BlockSpec((1, k, n), lambda i: (layer, 0, 0), pipeline_mode=pl.Buffered(1))


def _for_each_group(n_prompt_tiles, fn):
    i = pl.program_id(0)

    @pl.when(i < n_prompt_tiles)
    def _():
        fn(False)

    @pl.when(i >= n_prompt_tiles)
    def _():
        fn(True)


def _sub_blocks(tm, sub, is_sample):
    for r in range(tm // sub):
        rows = slice(r * sub, (r + 1) * sub)
        if is_sample:
            g = sub // SEQ_PAD
            yield rows, (g, SEQ_PAD, D_MODEL), slice(r * g, (r + 1) * g)
        else:
            yield rows, (1, sub, D_MODEL), slice(0, 1)


def _prenorm_kernel(xp_ref, xs_ref, gain_ref, scp_ref, shp_ref, scs_ref, shs_ref, h_ref, *, tiles, sub):
    def run(is_sample):
        x_ref, sc_ref, sh_ref = (xs_ref, scs_ref, shs_ref) if is_sample else (xp_ref, scp_ref, shp_ref)
        for rows, shape3, seqs in _sub_blocks(tiles.tm, sub, is_sample):
            h = _norm_mod(x_ref[rows, :].reshape(shape3), gain_ref[...], sc_ref[0, seqs], sh_ref[0, seqs])
            h_ref[rows, :] = h.reshape(sub, D_MODEL).astype(BF16)

    _for_each_group(tiles.n_prompt, run)


def _prenorm_call(tiles, layer, xp, xs, gains, mod4):
    return pl.pallas_call(
        functools.partial(_prenorm_kernel, tiles=tiles, sub=128),
        grid=(tiles.n_tiles,),
        in_specs=[
            tiles.prompt_rows(D_MODEL), tiles.sample_rows(D_MODEL), _gain_spec(layer, 0),
            tiles.prompt_mod(layer, MOD_SC1), tiles.prompt_mod(layer, MOD_SH1),
            tiles.sample_mod(layer, MOD_SC1), tiles.sample_mod(layer, MOD_SH1),
        ],
        out_specs=tiles.all_rows(D_MODEL),
        out_shape=jax.ShapeDtypeStruct((tiles.n_tiles * tiles.tm, D_MODEL), BF16),
        compiler_params=_params("arbitrary"),
        name="prenorm",
    )(xp, xs, gains, mod4, mod4, mod4, mod4)


def _inproj_kernel(h_ref, w_ref, o_ref, wb_scr):
    @pl.when(pl.program_id(1) == 0)
    def _():
        wb_scr[...] = w_ref[0].astype(BF16)

    o_ref[...] = jnp.dot(h_ref[...], wb_scr[...], preferred_element_type=F32).astype(o_ref.dtype)


def _inproj_call(layer, h, w_in, tm, tn):
    rows = h.shape[0]
    return pl.pallas_call(
        _inproj_kernel,
        grid=(PROJ_WIDTH // tn, rows // tm),
        in_specs=[
            pl.BlockSpec((tm, D_MODEL), lambda j, i: (i, 0)),
            pl.BlockSpec((1, D_MODEL, tn), lambda j, i: (layer, 0, j)),
        ],
        out_specs=pl.BlockSpec((tm, tn), lambda j, i: (i, j)),
        out_shape=jax.ShapeDtypeStruct((rows, PROJ_WIDTH), BF16),
        scratch_shapes=[pltpu.VMEM((D_MODEL, tn), BF16)],
        compiler_params=_params("arbitrary", "arbitrary"),
        name="in_proj",
    )(h, w_in)


def _gateup_kernel(h_ref, wg_ref, wu_ref, o_ref, wg_scr, wu_scr):
    @pl.when(pl.program_id(1) == 0)
    def _():
        wg_scr[...] = wg_ref[0].astype(BF16)
        wu_scr[...] = wu_ref[0].astype(BF16)

    h = h_ref[...]
    a = jnp.dot(h, wg_scr[...], preferred_element_type=F32)
    b = jnp.dot(h, wu_scr[...], preferred_element_type=F32)
    o_ref[...] = (_silu(a) * b).astype(o_ref.dtype)


def _gateup_call(layer, h2, w_gate, w_up, tm, tn):
    rows = h2.shape[0]
    w_spec = pl.BlockSpec((1, D_MODEL, tn), lambda j, i: (layer, 0, j))
    return pl.pallas_call(
        _gateup_kernel,
        grid=(D_FF // tn, rows // tm),
        in_specs=[pl.BlockSpec((tm, D_MODEL), lambda j, i: (i, 0)), w_spec, w_spec],
        out_specs=pl.BlockSpec((tm, tn), lambda j, i: (i, j)),
        out_shape=jax.ShapeDtypeStruct((rows, D_FF), BF16),
        scratch_shapes=[pltpu.VMEM((D_MODEL, tn), BF16), pltpu.VMEM((D_MODEL, tn), BF16)],
        compiler_params=_params("arbitrary", "arbitrary"),
        name="gate_up",
    )(h2, w_gate, w_up)


def _resid_kernel(*refs, tiles, sub, emit_h):
    if emit_h:
        (a_ref, w_ref, xp_ref, xs_ref, gain1_ref, gp_ref, gs_ref, gain2_ref, scp_ref, shp_ref, scs_ref, shs_ref,
         yp_ref, ys_ref, h_ref) = refs
    else:
        a_ref, w_ref, xp_ref, xs_ref, gain1_ref, gp_ref, gs_ref, yp_ref, ys_ref = refs

    def run(is_sample):
        x_ref, y_ref, g_ref = (xs_ref, ys_ref, gs_ref) if is_sample else (xp_ref, yp_ref, gp_ref)
        for rows, shape3, seqs in _sub_blocks(tiles.tm, sub, is_sample):
            o = jnp.dot(a_ref[rows, :], w_ref[0], preferred_element_type=F32).reshape(shape3)
            y = x_ref[rows, :].reshape(shape3) + g_ref[0, seqs] * _rms(o, gain1_ref[...])
            y_ref[rows, :] = y.reshape(sub, D_MODEL)
            if emit_h:
                sc_ref, sh_ref = (scs_ref, shs_ref) if is_sample else (scp_ref, shp_ref)
                h = _norm_mod(y, gain2_ref[...], sc_ref[0, seqs], sh_ref[0, seqs])
                h_ref[rows, :] = h.reshape(sub, D_MODEL).astype(BF16)

    _for_each_group(tiles.n_prompt, run)


def _resid_call(name, tiles, sub, a, w_b, xp, xs, gains, mod4, layer, k_dim, gain1, gate_chunk, h_params):
    emit_h = h_params is not None
    in_specs = [
        tiles.all_rows(k_dim), _const_weight_spec(layer, k_dim, D_MODEL),
        tiles.prompt_rows(D_MODEL), tiles.sample_rows(D_MODEL),
        _gain_spec(layer, gain1), tiles.prompt_mod(layer, gate_chunk), tiles.sample_mod(layer, gate_chunk),
    ]
    args = [a, w_b, xp, xs, gains, mod4, mod4]
    out_specs = [tiles.prompt_rows(D_MODEL), tiles.sample_rows(D_MODEL)]
    out_shape = [jax.ShapeDtypeStruct(xp.shape, F32), jax.ShapeDtypeStruct(xs.shape, F32)]
    if emit_h:
        hl, hg, hsc, hsh = h_params
        in_specs += [_gain_spec(hl, hg), tiles.prompt_mod(hl, hsc), tiles.prompt_mod(hl, hsh),
                     tiles.sample_mod(hl, hsc), tiles.sample_mod(hl, hsh)]
        args += [gains, mod4, mod4, mod4, mod4]
        out_specs.append(tiles.all_rows(D_MODEL))
        out_shape.append(jax.ShapeDtypeStruct((tiles.n_tiles * tiles.tm, D_MODEL), BF16))
    return pl.pallas_call(
        functools.partial(_resid_kernel, tiles=tiles, sub=sub, emit_h=emit_h),
        grid=(tiles.n_tiles,),
        in_specs=in_specs,
        out_specs=out_specs,
        out_shape=out_shape,
        compiler_params=_params("arbitrary"),
        name=name,
    )(*args)


def _mixer_tables(t_rows, t_valid):
    scale = RET_DK ** -0.5
    i = np.arange(t_rows)[:, None].astype(np.float64)
    j = np.arange(CHUNK)[None, :].astype(np.float64)
    jr = np.arange(CHUNK)[:, None].astype(np.float64)
    dtab = np.zeros((RET_HEADS, t_rows, CHUNK), np.float64)
    qtab = np.zeros((RET_HEADS, t_rows, CHUNK), np.float64)
    ktab = np.zeros((RET_HEADS, CHUNK, CHUNK), np.float64)
    sdec = []
    for h, lg in enumerate(LOG_GAMMA):
        causal = (i >= j) & (i < t_valid) & (j < t_valid)
        dtab[h] = np.where(causal, scale * np.exp(np.maximum(i - j, 0.0) * lg), 0.0)
        qtab[h] = np.where(i < t_valid, np.exp((i + 1.0) * lg), 0.0) * np.ones((1, CHUNK))
        ktab[h] = np.where(jr < t_valid, scale * np.exp(np.maximum(t_valid - 1.0 - jr, 0.0) * lg), 0.0) * np.ones((1, CHUNK))
        sdec.append(float(np.exp(t_valid * lg)))
    return (jnp.asarray(dtab, F32), jnp.asarray(qtab, F32), jnp.asarray(ktab, F32)), sdec


def _mixer_math(proj, state, k_prev, v_prev, u_prev, tabs, sdec, conv_w, sinks, min_key, mixed_ref, ubuf):
    dtab_ref, qtab_ref, ktab_ref = tabs
    t_rows = mixed_ref.shape[0]
    pad = CHUNK - t_rows

    def pad_rows(a):
        if pad == 0:
            return a
        return jnp.concatenate([a, jnp.zeros((pad, a.shape[1]), a.dtype)], axis=0)

    new_state = []
    for h in range(RET_HEADS):
        q = proj(OFF_RQ + h * RET_DK, OFF_RQ + (h + 1) * RET_DK).astype(BF16)
        k = pad_rows(proj(OFF_RK + h * RET_DK, OFF_RK + (h + 1) * RET_DK).astype(BF16))
        v = pad_rows(proj(OFF_RV + h * RET_DV, OFF_RV + (h + 1) * RET_DV).astype(BF16))
        gate = proj(OFF_RG + h * RET_DV, OFF_RG + (h + 1) * RET_DV).astype(F32)
        s = lax.dot_general(q, k, (((1,), (1,)), ((), ())), preferred_element_type=F32)
        p = (s * dtab_ref[h]).astype(BF16)
        intra = jnp.dot(p, v, preferred_element_type=F32)
        s0 = state[h]
        inter = jnp.dot(q, s0.astype(BF16), preferred_element_type=F32) * qtab_ref[h]
        ry = intra + inter
        ks = (k.astype(F32) * ktab_ref[h]).astype(BF16)
        kv = lax.dot_general(ks, v, (((0,), (0,)), ((), ())), preferred_element_type=F32)
        new_state.append(sdec[h] * s0 + kv)
        ryn = ry * lax.rsqrt(jnp.mean(ry * ry, axis=-1, keepdims=True) + EPS)
        mixed_ref[:, h * RET_DV:(h + 1) * RET_DV] = (_silu(gate) * ryn).astype(mixed_ref.dtype)

    k_cur = pad_rows(proj(OFF_AK, OFF_AK + 128).astype(BF16))
    v_cur = pad_rows(proj(OFF_AV, OFF_AV + 128).astype(BF16))
    k_all = jnp.concatenate([k_prev, k_cur], axis=0)
    v_all = jnp.concatenate([v_prev, v_cur], axis=0)
    qi = lax.broadcasted_iota(jnp.int32, (t_rows, 2 * CHUNK), 0)
    kj = lax.broadcasted_iota(jnp.int32, (t_rows, 2 * CHUNK), 1)
    dist = qi + CHUNK - kj
    allowed = (dist >= 0) & (dist < WINDOW)
    if min_key is not None:
        allowed = allowed & (kj >= min_key)
    dist_f = dist.astype(F32)
    att_base = RET_HEADS * RET_DV
    for hh in range(ATT_Q_HEADS):
        kvh = hh // ATT_GROUP
        lo, hi = kvh * ATT_HEAD_DIM, (kvh + 1) * ATT_HEAD_DIM
        q = proj(OFF_AQ + hh * ATT_HEAD_DIM, OFF_AQ + (hh + 1) * ATT_HEAD_DIM).astype(BF16)
        s = lax.dot_general(q, k_all[:, lo:hi], (((1,), (1,)), ((), ())), preferred_element_type=F32)
        s = s * (ATT_HEAD_DIM ** -0.5) - ALIBI_SLOPES[hh] * dist_f
        s = jnp.where(allowed, s, NEG_INF)
        sink = sinks[hh]
        m = jnp.maximum(jnp.max(s, axis=-1, keepdims=True), sink)
        e = jnp.exp(s - m)
        den = jnp.sum(e, axis=-1, keepdims=True) + jnp.exp(sink - m)
        p = (e / den).astype(BF16)
        o = jnp.dot(p, v_all[:, lo:hi], preferred_element_type=F32)
        c0 = att_base + hh * ATT_HEAD_DIM
        mixed_ref[:, c0:c0 + ATT_HEAD_DIM] = o.astype(mixed_ref.dtype)

    gate_b = proj(OFF_CB, OFF_CB + CONV_CH).astype(F32)
    u = proj(OFF_CC, OFF_CC + CONV_CH).astype(F32) * proj(OFF_CH, OFF_CH + CONV_CH).astype(F32)
    ubuf[6:8, :] = u_prev
    ubuf[8:8 + t_rows, :] = u
    y = conv_w[0:1, :] * ubuf[6:6 + t_rows, :] + conv_w[1:2, :] * ubuf[7:7 + t_rows, :] + conv_w[2:3, :] * u
    c0 = att_base + ATT_Q_HEADS * ATT_HEAD_DIM
    mixed_ref[:, c0:c0 + CONV_CH] = (gate_b * y).astype(mixed_ref.dtype)
    return new_state, k_cur, v_cur


def _mix_prompt_kernel(sink_ref, proj_ref, dtab_ref, qtab_ref, ktab_ref, convw_ref,
                       mixed_ref, sret_ref, knew_ref, vnew_ref, cnew_ref,
                       s_scr, kprev_scr, vprev_scr, uprev_scr, ubuf, *, layer, sdec):
    c = pl.program_id(1)

    @pl.when(c == 0)
    def _():
        s_scr[...] = jnp.zeros_like(s_scr)
        kprev_scr[...] = jnp.zeros_like(kprev_scr)
        vprev_scr[...] = jnp.zeros_like(vprev_scr)
        uprev_scr[...] = jnp.zeros_like(uprev_scr)

    proj = lambda lo, hi: proj_ref[:, lo:hi]
    sinks = [sink_ref[layer, hh] for hh in range(ATT_Q_HEADS)]
    state = [s_scr[h] for h in range(RET_HEADS)]
    new_state, k_cur, v_cur = _mixer_math(
        proj, state, kprev_scr[...], vprev_scr[...], uprev_scr[...], (dtab_ref, qtab_ref, ktab_ref), sdec,
        convw_ref[0], sinks, jnp.where(c == 0, CHUNK, 0), mixed_ref, ubuf)
    for h in range(RET_HEADS):
        s_scr[h] = new_state[h]
    kprev_scr[...] = k_cur
    vprev_scr[...] = v_cur
    uprev_scr[...] = ubuf[CHUNK + 6:CHUNK + 8, :]

    @pl.when(c == pl.num_programs(1) - 1)
    def _():
        for h in range(RET_HEADS):
            sret_ref[0, h] = new_state[h]
        knew_ref[0] = k_cur.astype(F32)
        vnew_ref[0] = v_cur.astype(F32)
        cnew_ref[0] = ubuf[CHUNK + 6:CHUNK + 8, :]


def _mix_prompt_call(layer, proj, batch, seq, conv_w, attn_sinks, tables, sdec):
    n_chunks = seq // CHUNK
    const3 = lambda bi, ci: (0, 0, 0)
    return pl.pallas_call(
        functools.partial(_mix_prompt_kernel, layer=layer, sdec=sdec),
        grid=(batch, n_chunks),
        in_specs=[
            pl.BlockSpec(memory_space=pltpu.SMEM),
            pl.BlockSpec((CHUNK, PROJ_WIDTH), lambda bi, ci: (bi * n_chunks + ci, 0)),
            pl.BlockSpec((RET_HEADS, CHUNK, CHUNK), const3),
            pl.BlockSpec((RET_HEADS, CHUNK, CHUNK), const3),
            pl.BlockSpec((RET_HEADS, CHUNK, CHUNK), const3),
            pl.BlockSpec((1, CONV_WIDTH, CONV_CH), lambda bi, ci: (layer, 0, 0)),
        ],
        out_specs=[
            pl.BlockSpec((CHUNK, D_MODEL), lambda bi, ci: (bi * n_chunks + ci, 0)),
            pl.BlockSpec((1, RET_HEADS, RET_DK, RET_DV), lambda bi, ci: (bi, 0, 0, 0)),
            pl.BlockSpec((1, WINDOW, 128), lambda bi, ci: (bi, 0, 0)),
            pl.BlockSpec((1, WINDOW, 128), lambda bi, ci: (bi, 0, 0)),
            pl.BlockSpec((1, CONV_WIDTH - 1, CONV_CH), lambda bi, ci: (bi, 0, 0)),
        ],
        out_shape=[
            jax.ShapeDtypeStruct((proj.shape[0], D_MODEL), BF16),
            jax.ShapeDtypeStruct((batch, RET_HEADS, RET_DK, RET_DV), F32),
            jax.ShapeDtypeStruct((batch, WINDOW, 128), F32),
            jax.ShapeDtypeStruct((batch, WINDOW, 128), F32),
            jax.ShapeDtypeStruct((batch, CONV_WIDTH - 1, CONV_CH), F32),
        ],
        scratch_shapes=[
            pltpu.VMEM((RET_HEADS, RET_DK, RET_DV), F32),
            pltpu.VMEM((CHUNK, 128), BF16),
            pltpu.VMEM((CHUNK, 128), BF16),
            pltpu.VMEM((CONV_WIDTH - 1, CONV_CH), F32),
            pltpu.VMEM((CHUNK + 8, CONV_CH), F32),
        ],
        compiler_params=_params("arbitrary", "arbitrary"),
        name="mix_prompt",
    )(attn_sinks, proj, *tables, conv_w)


def _mix_sample_kernel(sink_ref, proj_ref, dtab_ref, qtab_ref, ktab_ref, convw_ref,
                       sin_ref, kbuf_ref, vbuf_ref, cbuf_ref, mixed_in_ref,
                       mixed_ref, sret_ref, knew_ref, vnew_ref, cnew_ref,
                       ubuf, kvbuf, *, layer, sdec, t_valid):
    del mixed_in_ref
    proj = lambda lo, hi: proj_ref[:, lo:hi]
    sinks = [sink_ref[layer, hh] for hh in range(ATT_Q_HEADS)]
    state = [sin_ref[0, h] for h in range(RET_HEADS)]
    new_state, k_cur, v_cur = _mixer_math(
        proj, state, kbuf_ref[0].astype(BF16), vbuf_ref[0].astype(BF16), cbuf_ref[0],
        (dtab_ref, qtab_ref, ktab_ref), sdec, convw_ref[0], sinks, None, mixed_ref, ubuf)
    for h in range(RET_HEADS):
        sret_ref[0, h] = new_state[h]
    t_rows = mixed_ref.shape[0]
    for src_ref, off, dst_ref in ((kbuf_ref, OFF_AK, knew_ref), (vbuf_ref, OFF_AV, vnew_ref)):
        kvbuf[0:WINDOW, :] = src_ref[0]
        kvbuf[WINDOW:WINDOW + t_rows, :] = proj(off, off + 128).astype(F32)
        dst_ref[0] = kvbuf[t_valid:t_valid + WINDOW, :]
    cnew_ref[0] = ubuf[8 + t_valid - 2:8 + t_valid, :]


def _mix_sample_call(layer, proj, mixed, row0, conv_w, attn_sinks, tables, sdec, s_in, kbuf, vbuf, cbuf, t_valid):
    nb = s_in.shape[0]
    blk0 = row0 // SEQ_PAD
    const3 = lambda bi: (0, 0, 0)
    per_seq3 = lambda bi: (bi, 0, 0)
    per_seq4 = lambda bi: (bi, 0, 0, 0)
    return pl.pallas_call(
        functools.partial(_mix_sample_kernel, layer=layer, sdec=sdec, t_valid=t_valid),
        grid=(nb,),
        in_specs=[
            pl.BlockSpec(memory_space=pltpu.SMEM),
            pl.BlockSpec((SEQ_PAD, PROJ_WIDTH), lambda bi: (blk0 + bi, 0)),
            pl.BlockSpec((RET_HEADS, SEQ_PAD, CHUNK), const3),
            pl.BlockSpec((RET_HEADS, SEQ_PAD, CHUNK), const3),
            pl.BlockSpec((RET_HEADS, CHUNK, CHUNK), const3),
            pl.BlockSpec((1, CONV_WIDTH, CONV_CH), lambda bi: (layer, 0, 0)),
            pl.BlockSpec((1, RET_HEADS, RET_DK, RET_DV), per_seq4),
            pl.BlockSpec((1, WINDOW, 128), per_seq3),
            pl.BlockSpec((1, WINDOW, 128), per_seq3),
            pl.BlockSpec((1, CONV_WIDTH - 1, CONV_CH), per_seq3),
            pl.BlockSpec(memory_space=pl.ANY),
        ],
        out_specs=[
            pl.BlockSpec((SEQ_PAD, D_MODEL), lambda bi: (blk0 + bi, 0)),
            pl.BlockSpec((1, RET_HEADS, RET_DK, RET_DV), per_seq4),
            pl.BlockSpec((1, WINDOW, 128), per_seq3),
            pl.BlockSpec((1, WINDOW, 128), per_seq3),
            pl.BlockSpec((1, CONV_WIDTH - 1, CONV_CH), per_seq3),
        ],
        out_shape=[
            jax.ShapeDtypeStruct(mixed.shape, BF16),
            jax.ShapeDtypeStruct((nb, RET_HEADS, RET_DK, RET_DV), F32),
            jax.ShapeDtypeStruct((nb, WINDOW, 128), F32),
            jax.ShapeDtypeStruct((nb, WINDOW, 128), F32),
            jax.ShapeDtypeStruct((nb, CONV_WIDTH - 1, CONV_CH), F32),
        ],
        scratch_shapes=[
            pltpu.VMEM((SEQ_PAD + 8, CONV_CH), F32),
            pltpu.VMEM((WINDOW + SEQ_PAD, 128), F32),
        ],
        input_output_aliases={10: 0},
        compiler_params=_params("arbitrary"),
        name="mix_sample",
    )(attn_sinks, proj, *tables, conv_w, s_in, kbuf, vbuf, cbuf, mixed)


def kernel(x_prompt, x_sample, state_ret, cache_win_k, cache_win_v, state_conv, c_prompt, c_sample,
           w_in, w_out, conv_w, attn_sinks, norm_g, w_ada, b_ada, w_ff_gate, w_ff_up, w_ff_down):
    batch, seq, _ = x_prompt.shape
    nb, t_valid, _ = x_sample.shape
    w_buf = cache_win_k.shape[2]
    assert w_buf == WINDOW and seq % CHUNK == 0 and t_valid <= SEQ_PAD and nb + batch <= MOD_ROWS
    rows_p, rows_s = batch * seq, nb * SEQ_PAD
    rows = rows_p + rows_s

    w_out_b, w_down_b = w_out.astype(BF16), w_ff_down.astype(BF16)

    c_all = jnp.concatenate([c_sample, c_prompt, jnp.zeros((MOD_ROWS - nb - batch, D_MODEL), F32)], axis=0)
    mod4 = _ada_call(c_all, w_ada, b_ada).reshape(DEPTH, MOD_ROWS, 1, N_MOD * D_MODEL)
    gains = norm_g.reshape(DEPTH * 4, 1, D_MODEL)

    tiles_out = _RowTiles(512, rows_p, rows_s, seq, nb)
    tiles_down = _RowTiles(256, rows_p, rows_s, seq, nb)
    tm_stream = rows // 8
    assert rows % 8 == 0 and tm_stream % SEQ_PAD == 0
    tabs_p, sdec_p = _mixer_tables(CHUNK, CHUNK)
    tabs_s, sdec_s = _mixer_tables(SEQ_PAD, t_valid)

    xp = x_prompt.reshape(rows_p, D_MODEL)
    xs = jnp.pad(x_sample, ((0, 0), (0, SEQ_PAD - t_valid), (0, 0))).reshape(rows_s, D_MODEL)
    kbuf = cache_win_k.reshape(DEPTH, nb, w_buf, 128)
    vbuf = cache_win_v.reshape(DEPTH, nb, w_buf, 128)

    h = _prenorm_call(tiles_out, 0, xp, xs, gains, mod4)
    outs_p, outs_s = [], []
    for l in range(DEPTH):
        proj = _inproj_call(l, h, w_in, tm_stream, 1280)
        mixed, *st_p = _mix_prompt_call(l, proj, batch, seq, conv_w, attn_sinks, tabs_p, sdec_p)
        mixed, *st_s = _mix_sample_call(l, proj, mixed, rows_p, conv_w, attn_sinks, tabs_s, sdec_s,
                                        state_ret[l], kbuf[l], vbuf[l], state_conv[l], t_valid)
        xp, xs, h2 = _resid_call("out_proj", tiles_out, 256, mixed, w_out_b, xp, xs, gains, mod4, l, D_MODEL,
                                 1, MOD_G1, (l, 2, MOD_SC2, MOD_SH2))
        hid = _gateup_call(l, h2, w_ff_gate, w_ff_up, tm_stream, 512)
        next_h = (l + 1, 0, MOD_SC1, MOD_SH1) if l + 1 < DEPTH else None
        res = _resid_call("ffn_down", tiles_down, 256, hid, w_down_b, xp, xs, gains, mod4, l, D_FF,
                          3, MOD_G2, next_h)
        xp, xs = res[0], res[1]
        if next_h is not None:
            h = res[2]
        outs_p.append(st_p)
        outs_s.append(st_s)

    def stack(outs, idx, shape):
        return jnp.stack([o[idx] for o in outs]).reshape(shape)

    kv_shape_p = (DEPTH, batch, w_buf, ATT_KV_HEADS, ATT_HEAD_DIM)
    kv_shape_s = (DEPTH, nb, w_buf, ATT_KV_HEADS, ATT_HEAD_DIM)
    return (
        xp.reshape(batch, seq, D_MODEL),
        xs.reshape(nb, SEQ_PAD, D_MODEL)[:, :t_valid],
        stack(outs_p, 0, (DEPTH, batch, RET_HEADS, RET_DK, RET_DV)),
        stack(outs_s, 0, (DEPTH, nb, RET_HEADS, RET_DK, RET_DV)),
        stack(outs_p, 1, kv_shape_p),
        stack(outs_s, 1, kv_shape_s),
        stack(outs_p, 2, kv_shape_p),
        stack(outs_s, 2, kv_shape_s),
        stack(outs_p, 3, (DEPTH, batch, CONV_WIDTH - 1, CONV_CH)),
        stack(outs_s, 3, (DEPTH, nb, CONV_WIDTH - 1, CONV_CH)),
    )
```

```python
import functools
import math

import numpy as np
import jax
import jax.numpy as jnp
from jax import lax
from jax.experimental import pallas as pl
from jax.experimental.pallas import tpu as pltpu

F32 = jnp.float32
BF16 = jnp.bfloat16

D_MODEL = 2048
DEPTH = 4
RET_DK = 128
RET_DV = 128
RET_HEADS = 8
CHUNK = 128
ATT_HEAD_DIM = 64
ATT_Q_HEADS = 8
ATT_KV_HEADS = 2
ATT_GROUP = 4
WINDOW = 128
CONV_WIDTH = 3
CONV_CH = 512
D_FF = 5632
N_MOD = 6
EPS = 1e-6
NEG_INF = -1e30
PROJ_WIDTH = 6400
OFF_RQ, OFF_RK, OFF_RV, OFF_RG = 0, 1024, 2048, 3072
OFF_AQ, OFF_AK, OFF_AV = 4096, 4608, 4736
OFF_CB, OFF_CC, OFF_CH = 4864, 5376, 5888
MOD_SH1, MOD_SC1, MOD_G1, MOD_SH2, MOD_SC2, MOD_G2 = range(6)

SEQ_PAD = 16
MOD_ROWS = 48
VMEM_LIMIT_V7X = 56 * 1024 * 1024

LOG_GAMMA = [math.log1p(-(2.0 ** (-5.0 - h))) for h in range(RET_HEADS)]
ALIBI_SLOPES = [2.0 ** (-8.0 * (h + 1) / ATT_Q_HEADS) for h in range(ATT_Q_HEADS)]


def _silu(a):
    return a / (1.0 + jnp.exp(-a))


def _rms(x, gain):
    return x * lax.rsqrt(jnp.mean(x * x, axis=-1, keepdims=True) + EPS) * gain


def _norm_mod(x, gain, scale, shift):
    return _rms(x, gain) * (1.0 + scale) + shift


def _params(*sem):
    return pltpu.CompilerParams(dimension_semantics=sem, vmem_limit_bytes=VMEM_LIMIT_V7X)


def _ada_kernel(c_ref, w_ref, b_ref, o_ref):
    a = _silu(c_ref[...]).astype(BF16)
    o_ref[0] = jnp.dot(a, w_ref[0].astype(BF16), preferred_element_type=F32) + b_ref[0]


def _ada_call(c_all, w_ada, b_ada):
    tn = 1024
    n = N_MOD * D_MODEL
    return pl.pallas_call(
        _ada_kernel,
        grid=(DEPTH, n // tn),
        in_specs=[
            pl.BlockSpec((MOD_ROWS, D_MODEL), lambda l, j: (0, 0)),
            pl.BlockSpec((1, D_MODEL, tn), lambda l, j: (l, 0, j)),
            pl.BlockSpec((1, 1, tn), lambda l, j: (l, 0, j)),
        ],
        out_specs=pl.BlockSpec((1, MOD_ROWS, tn), lambda l, j: (l, 0, j)),
        out_shape=jax.ShapeDtypeStruct((DEPTH, MOD_ROWS, n), F32),
        compiler_params=_params("arbitrary", "arbitrary"),
        name="ada_mod",
    )(c_all, w_ada, b_ada.reshape(DEPTH, 1, n))


class _RowTiles:
    def __init__(self, tm, rows_prompt, rows_sample, seq_len, n_seq):
        assert rows_prompt % tm == 0 and rows_sample % tm == 0 and seq_len % tm == 0 and tm % SEQ_PAD == 0
        self.tm = tm
        self.n_prompt = rows_prompt // tm
        self.n_sample = rows_sample // tm
        self.n_tiles = self.n_prompt + self.n_sample
        self.tiles_per_seq = seq_len // tm
        self.seqs_per_tile = tm // SEQ_PAD
        self.prompt_mod_row0 = n_seq

    def all_rows(self, width):
        return pl.BlockSpec((self.tm, width), lambda i: (i, 0))

    def prompt_rows(self, width):
        last = self.n_prompt - 1
        return pl.BlockSpec((self.tm, width), lambda i: (jnp.minimum(i, last), 0))

    def sample_rows(self, width):
        first = self.n_prompt
        return pl.BlockSpec((self.tm, width), lambda i: (jnp.maximum(i - first, 0), 0))

    def prompt_mod(self, layer, chunk):
        last, per, row0 = self.n_prompt - 1, self.tiles_per_seq, self.prompt_mod_row0
        return pl.BlockSpec((1, 1, 1, D_MODEL), lambda i: (layer, row0 + jnp.minimum(i, last) // per, 0, chunk))

    def sample_mod(self, layer, chunk):
        first = self.n_prompt
        return pl.BlockSpec((1, self.seqs_per_tile, 1, D_MODEL),
                            lambda i: (layer, jnp.maximum(i - first, 0), 0, chunk))


def _gain_spec(layer, which):
    return pl.BlockSpec((1, 1, D_MODEL), lambda i: (layer * 4 + which, 0, 0))


def _const_weight_spec(layer, k, n):
    return pl.BlockSpec((1, k, n), lambda i: (layer, 0, 0), pipeline_mode=pl.Buffered(1))


def _for_each_group(n_prompt_tiles, fn):
    i = pl.program_id(0)

    @pl.when(i < n_prompt_tiles)
    def _():
        fn(False)

    @pl.when(i >= n_prompt_tiles)
    def _():
        fn(True)


def _sub_blocks(tm, sub, is_sample):
    for r in range(tm // sub):
        rows = slice(r * sub, (r + 1) * sub)
        if is_sample:
            g = sub // SEQ_PAD
            yield rows, (g, SEQ_PAD, D_MODEL), slice(r * g, (r + 1) * g)
        else:
            yield rows, (1, sub, D_MODEL), slice(0, 1)


def _prenorm_kernel(xp_ref, xs_ref, gain_ref, scp_ref, shp_ref, scs_ref, shs_ref, h_ref, *, tiles, sub):
    def run(is_sample):
        x_ref, sc_ref, sh_ref = (xs_ref, scs_ref, shs_ref) if is_sample else (xp_ref, scp_ref, shp_ref)
        for rows, shape3, seqs in _sub_blocks(tiles.tm, sub, is_sample):
            h = _norm_mod(x_ref[rows, :].reshape(shape3), gain_ref[...], sc_ref[0, seqs], sh_ref[0, seqs])
            h_ref[rows, :] = h.reshape(sub, D_MODEL).astype(BF16)

    _for_each_group(tiles.n_prompt, run)


def _prenorm_call(tiles, layer, xp, xs, gains, mod4):
    return pl.pallas_call(
        functools.partial(_prenorm_kernel, tiles=tiles, sub=128),
        grid=(tiles.n_tiles,),
        in_specs=[
            tiles.prompt_rows(D_MODEL), tiles.sample_rows(D_MODEL), _gain_spec(layer, 0),
            tiles.prompt_mod(layer, MOD_SC1), tiles.prompt_mod(layer, MOD_SH1),
            tiles.sample_mod(layer, MOD_SC1), tiles.sample_mod(layer, MOD_SH1),
        ],
        out_specs=tiles.all_rows(D_MODEL),
        out_shape=jax.ShapeDtypeStruct((tiles.n_tiles * tiles.tm, D_MODEL), BF16),
        compiler_params=_params("arbitrary"),
        name="prenorm",
    )(xp, xs, gains, mod4, mod4, mod4, mod4)


def _inproj_kernel(h_ref, w_ref, o_ref, wb_scr):
    @pl.when(pl.program_id(1) == 0)
    def _():
        wb_scr[...] = w_ref[0].astype(BF16)

    o_ref[...] = jnp.dot(h_ref[...], wb_scr[...], preferred_element_type=F32).astype(o_ref.dtype)


def _inproj_call(layer, h, w_in, tm, tn):
    rows = h.shape[0]
    return pl.pallas_call(
        _inproj_kernel,
        grid=(PROJ_WIDTH // tn, rows // tm),
        in_specs=[
            pl.BlockSpec((tm, D_MODEL), lambda j, i: (i, 0)),
            pl.BlockSpec((1, D_MODEL, tn), lambda j, i: (layer, 0, j)),
        ],
        out_specs=pl.BlockSpec((tm, tn), lambda j, i: (i, j)),
        out_shape=jax.ShapeDtypeStruct((rows, PROJ_WIDTH), BF16),
        scratch_shapes=[pltpu.VMEM((D_MODEL, tn), BF16)],
        compiler_params=_params("arbitrary", "arbitrary"),
        name="in_proj",
    )(h, w_in)


def _gateup_kernel(h_ref, wg_ref, wu_ref, o_ref, wg_scr, wu_scr):
    @pl.when(pl.program_id(1) == 0)
    def _():
        wg_scr[...] = wg_ref[0].astype(BF16)
        wu_scr[...] = wu_ref[0].astype(BF16)

    h = h_ref[...]
    a = jnp.dot(h, wg_scr[...], preferred_element_type=F32)
    b = jnp.dot(h, wu_scr[...], preferred_element_type=F32)
    o_ref[...] = (_silu(a) * b).astype(o_ref.dtype)


def _gateup_call(layer, h2, w_gate, w_up, tm, tn):
    rows = h2.shape[0]
    w_spec = pl.BlockSpec((1, D_MODEL, tn), lambda j, i: (layer, 0, j))
    return pl.pallas_call(
        _gateup_kernel,
        grid=(D_FF // tn, rows // tm),
        in_specs=[pl.BlockSpec((tm, D_MODEL), lambda j, i: (i, 0)), w_spec, w_spec],
        out_specs=pl.BlockSpec((tm, tn), lambda j, i: (i, j)),
        out_shape=jax.ShapeDtypeStruct((rows, D_FF), BF16),
        scratch_shapes=[pltpu.VMEM((D_MODEL, tn), BF16), pltpu.VMEM((D_MODEL, tn), BF16)],
        compiler_params=_params("arbitrary", "arbitrary"),
        name="gate_up",
    )(h2, w_gate, w_up)


def _resid_kernel(*refs, tiles, sub, emit_h, a_split):
    refs = list(refs)
    ap_ref = refs.pop(0)
    as_ref = refs.pop(0) if a_split else ap_ref
    if emit_h:
        (w_ref, xp_ref, xs_ref, gain1_ref, gp_ref, gs_ref, gain2_ref, scp_ref, shp_ref, scs_ref, shs_ref,
         yp_ref, ys_ref, h_ref) = refs
    else:
        w_ref, xp_ref, xs_ref, gain1_ref, gp_ref, gs_ref, yp_ref, ys_ref = refs

    def run(is_sample):
        a_ref, x_ref, y_ref, g_ref = (as_ref, xs_ref, ys_ref, gs_ref) if is_sample else (ap_ref, xp_ref, yp_ref, gp_ref)
        for rows, shape3, seqs in _sub_blocks(tiles.tm, sub, is_sample):
            o = jnp.dot(a_ref[rows, :], w_ref[0], preferred_element_type=F32).reshape(shape3)
            gated_gain = gain1_ref[...] * g_ref[0, seqs]
            y = x_ref[rows, :].reshape(shape3) + o * lax.rsqrt(jnp.mean(o * o, axis=-1, keepdims=True) + EPS) * gated_gain
            y_ref[rows, :] = y.reshape(sub, D_MODEL)
            if emit_h:
                sc_ref, sh_ref = (scs_ref, shs_ref) if is_sample else (scp_ref, shp_ref)
                scaled_gain = gain2_ref[...] * (1.0 + sc_ref[0, seqs])
                h = y * lax.rsqrt(jnp.mean(y * y, axis=-1, keepdims=True) + EPS) * scaled_gain + sh_ref[0, seqs]
                h_ref[rows, :] = h.reshape(sub, D_MODEL).astype(BF16)

    _for_each_group(tiles.n_prompt, run)


def _resid_call(name, tiles, sub, a, w_b, xp, xs, gains, mod4, layer, gain1, gate_chunk, h_params):
    emit_h = h_params is not None
    a_split = isinstance(a, tuple)
    k_dim = w_b.shape[1]
    if a_split:
        in_specs, args = [tiles.prompt_rows(k_dim), tiles.sample_rows(k_dim)], list(a)
    else:
        in_specs, args = [tiles.all_rows(k_dim)], [a]
    in_specs += [
        _const_weight_spec(layer, k_dim, D_MODEL),
        tiles.prompt_rows(D_MODEL), tiles.sample_rows(D_MODEL),
        _gain_spec(layer, gain1), tiles.prompt_mod(layer, gate_chunk), tiles.sample_mod(layer, gate_chunk),
    ]
    args += [w_b, xp, xs, gains, mod4, mod4]
    out_specs = [tiles.prompt_rows(D_MODEL), tiles.sample_rows(D_MODEL)]
    out_shape = [jax.ShapeDtypeStruct(xp.shape, F32), jax.ShapeDtypeStruct(xs.shape, F32)]
    if emit_h:
        hl, hg, hsc, hsh = h_params
        in_specs += [_gain_spec(hl, hg), tiles.prompt_mod(hl, hsc), tiles.prompt_mod(hl, hsh),
                     tiles.sample_mod(hl, hsc), tiles.sample_mod(hl, hsh)]
        args += [gains, mod4, mod4, mod4, mod4]
        out_specs.append(tiles.all_rows(D_MODEL))
        out_shape.append(jax.ShapeDtypeStruct((tiles.n_tiles * tiles.tm, D_MODEL), BF16))
    return pl.pallas_call(
        functools.partial(_resid_kernel, tiles=tiles, sub=sub, emit_h=emit_h, a_split=a_split),
        grid=(tiles.n_tiles,),
        in_specs=in_specs,
        out_specs=out_specs,
        out_shape=out_shape,
        compiler_params=_params("arbitrary"),
        name=name,
    )(*args)


def _mixer_tables(t_rows, t_valid, first_chunk_variant):
    qi = np.arange(t_rows)[:, None]
    kj = np.arange(2 * CHUNK)[None, :]
    dist = qi + CHUNK - kj
    allowed = (dist >= 0) & (dist < WINDOW)
    variants = [allowed & (kj >= CHUNK), allowed] if first_chunk_variant else [allowed]
    bias = np.stack([np.stack([np.where(ok, -slope * dist.astype(np.float64), NEG_INF) for slope in ALIBI_SLOPES])
                     for ok in variants])
    scale = RET_DK ** -0.5
    i = np.arange(t_rows)[:, None].astype(np.float64)
    j = np.arange(CHUNK)[None, :].astype(np.float64)
    jr = np.arange(CHUNK)[:, None].astype(np.float64)
    dtab = np.zeros((RET_HEADS, t_rows, CHUNK), np.float64)
    qtab = np.zeros((RET_HEADS, t_rows, CHUNK), np.float64)
    ktab = np.zeros((RET_HEADS, CHUNK, CHUNK), np.float64)
    sdec = []
    for h, lg in enumerate(LOG_GAMMA):
        causal = (i >= j) & (i < t_valid) & (j < t_valid)
        dtab[h] = np.where(causal, scale * np.exp(np.maximum(i - j, 0.0) * lg), 0.0)
        qtab[h] = np.where(i < t_valid, np.exp((i + 1.0) * lg), 0.0) * np.ones((1, CHUNK))
        ktab[h] = np.where(jr < t_valid, scale * np.exp(np.maximum(t_valid - 1.0 - jr, 0.0) * lg), 0.0) * np.ones((1, CHUNK))
        sdec.append(float(np.exp(t_valid * lg)))
    return tuple(jnp.asarray(t, F32) for t in (dtab, qtab, ktab, bias)), sdec


def _mixer_math(proj, state, k_prev, v_prev, u_prev, tabs, bias_variant, sdec, conv_w, sinks, mixed_ref, ubuf):
    dtab_ref, qtab_ref, ktab_ref, bias_ref = tabs
    t_rows = mixed_ref.shape[0]
    pad = CHUNK - t_rows

    def pad_rows(a):
        if pad == 0:
            return a
        return jnp.concatenate([a, jnp.zeros((pad, a.shape[1]), a.dtype)], axis=0)

    nt = (((1,), (1,)), ((), ()))
    tn = (((0,), (0,)), ((), ()))
    att_base = RET_HEADS * RET_DV
    att_scale = ATT_HEAD_DIM ** -0.5
    k_cur = pad_rows(proj(OFF_AK, OFF_AK + 128))
    v_cur = pad_rows(proj(OFF_AV, OFF_AV + 128))
    k_all = jnp.concatenate([k_prev, k_cur], axis=0)
    v_all = jnp.concatenate([v_prev, v_cur], axis=0)
    kv_lanes = [slice(g * ATT_HEAD_DIM, (g + 1) * ATT_HEAD_DIM) for g in range(ATT_KV_HEADS)]

    rq = [proj(OFF_RQ + h * RET_DK, OFF_RQ + (h + 1) * RET_DK) for h in range(RET_HEADS)]
    rk = [pad_rows(proj(OFF_RK + h * RET_DK, OFF_RK + (h + 1) * RET_DK)) for h in range(RET_HEADS)]
    rv = [pad_rows(proj(OFF_RV + h * RET_DV, OFF_RV + (h + 1) * RET_DV)) for h in range(RET_HEADS)]
    ret_s = [lax.dot_general(rq[h], rk[h], nt, preferred_element_type=F32) for h in range(RET_HEADS)]
    inter = [jnp.dot(rq[h], state[h].astype(BF16), preferred_element_type=F32) for h in range(RET_HEADS)]
    att_s = []
    for hh in range(ATT_Q_HEADS):
        q = proj(OFF_AQ + hh * ATT_HEAD_DIM, OFF_AQ + (hh + 1) * ATT_HEAD_DIM) * att_scale
        att_s.append(lax.dot_general(q, k_all[:, kv_lanes[hh // ATT_GROUP]], nt, preferred_element_type=F32))
    new_state = []
    for h in range(RET_HEADS):
        ks = (rk[h].astype(F32) * ktab_ref[h]).astype(BF16)
        kv = lax.dot_general(ks, rv[h], tn, preferred_element_type=F32)
        new_state.append(sdec[h] * state[h] + kv)

    ret_p = [(ret_s[h] * dtab_ref[h]).astype(BF16) for h in range(RET_HEADS)]
    att_e, att_den = [], []
    for hh in range(ATT_Q_HEADS):
        s = att_s[hh] + bias_ref[bias_variant, hh]
        sink = sinks[hh]
        m = jnp.maximum(jnp.max(s, axis=-1, keepdims=True), sink)
        e = jnp.exp(s - m)
        att_den.append(jnp.sum(e, axis=-1, keepdims=True) + jnp.exp(sink - m))
        att_e.append(e.astype(BF16))

    intra = [jnp.dot(ret_p[h], rv[h], preferred_element_type=F32) for h in range(RET_HEADS)]
    att_o = [jnp.dot(att_e[hh], v_all[:, kv_lanes[hh // ATT_GROUP]], preferred_element_type=F32)
             for hh in range(ATT_Q_HEADS)]

    for h in range(RET_HEADS):
        gate = proj(OFF_RG + h * RET_DV, OFF_RG + (h + 1) * RET_DV).astype(F32)
        ry = intra[h] + inter[h] * qtab_ref[h]
        ryn = ry * lax.rsqrt(jnp.mean(ry * ry, axis=-1, keepdims=True) + EPS)
        mixed_ref[:, h * RET_DV:(h + 1) * RET_DV] = (_silu(gate) * ryn).astype(mixed_ref.dtype)

    for hh in range(ATT_Q_HEADS):
        c0 = att_base + hh * ATT_HEAD_DIM
        mixed_ref[:, c0:c0 + ATT_HEAD_DIM] = (att_o[hh] / att_den[hh]).astype(mixed_ref.dtype)

    gate_b = proj(OFF_CB, OFF_CB + CONV_CH).astype(F32)
    u = proj(OFF_CC, OFF_CC + CONV_CH).astype(F32) * proj(OFF_CH, OFF_CH + CONV_CH).astype(F32)
    ubuf[6:8, :] = u_prev
    ubuf[8:8 + t_rows, :] = u
    y = conv_w[0:1, :] * ubuf[6:6 + t_rows, :] + conv_w[1:2, :] * ubuf[7:7 + t_rows, :] + conv_w[2:3, :] * u
    c0 = att_base + ATT_Q_HEADS * ATT_HEAD_DIM
    mixed_ref[:, c0:c0 + CONV_CH] = (gate_b * y).astype(mixed_ref.dtype)
    return new_state, k_cur, v_cur


def _mix_prompt_kernel(sink_ref, *refs, layer, sdec, batch):
    proj_refs, refs = refs[:batch], refs[batch:]
    (dtab_ref, qtab_ref, ktab_ref, bias_ref, convw_ref, mixed_ref, sret_ref, knew_ref, vnew_ref, cnew_ref,
     s_scr, kprev_scr, vprev_scr, uprev_scr, ubuf) = refs
    c = pl.program_id(0)

    @pl.when(c == 0)
    def _():
        s_scr[...] = jnp.zeros_like(s_scr)
        kprev_scr[...] = jnp.zeros_like(kprev_scr)
        vprev_scr[...] = jnp.zeros_like(vprev_scr)
        uprev_scr[...] = jnp.zeros_like(uprev_scr)

    sinks = [sink_ref[layer, hh] for hh in range(ATT_Q_HEADS)]
    bias_variant = jnp.where(c == 0, 0, 1)
    last = c == pl.num_programs(0) - 1
    for b in range(batch):
        proj = functools.partial(lambda ref, lo, hi: ref[:, lo:hi], proj_refs[b])
        state = [s_scr[b, h] for h in range(RET_HEADS)]
        new_state, k_cur, v_cur = _mixer_math(
            proj, state, kprev_scr[b], vprev_scr[b], uprev_scr[b], (dtab_ref, qtab_ref, ktab_ref, bias_ref),
            bias_variant, sdec, convw_ref[0], sinks, mixed_ref.at[b], ubuf.at[b])
        for h in range(RET_HEADS):
            s_scr[b, h] = new_state[h]
        kprev_scr[b] = k_cur
        vprev_scr[b] = v_cur
        uprev_scr[b] = ubuf[b, CHUNK + 6:CHUNK + 8, :]

        @pl.when(last)
        def _():
            for h in range(RET_HEADS):
                sret_ref[b, h] = new_state[h]
            knew_ref[b] = k_cur.astype(F32)
            vnew_ref[b] = v_cur.astype(F32)
            cnew_ref[b] = ubuf[b, CHUNK + 6:CHUNK + 8, :]


def _mix_prompt_call(layer, proj, batch, seq, conv_w, attn_sinks, tables, sdec):
    n_chunks = seq // CHUNK
    const3 = lambda ci: (0, 0, 0)
    const4 = lambda ci: (0, 0, 0, 0)
    proj_specs = [pl.BlockSpec((CHUNK, PROJ_WIDTH), functools.partial(lambda b, ci: (b * n_chunks + ci, 0), b))
                  for b in range(batch)]
    return pl.pallas_call(
        functools.partial(_mix_prompt_kernel, layer=layer, sdec=sdec, batch=batch),
        grid=(n_chunks,),
        in_specs=[pl.BlockSpec(memory_space=pltpu.SMEM)] + proj_specs + [
            pl.BlockSpec((RET_HEADS, CHUNK, CHUNK), const3),
            pl.BlockSpec((RET_HEADS, CHUNK, CHUNK), const3),
            pl.BlockSpec((RET_HEADS, CHUNK, CHUNK), const3),
            pl.BlockSpec((2, ATT_Q_HEADS, CHUNK, 2 * CHUNK), const4),
            pl.BlockSpec((1, CONV_WIDTH, CONV_CH), lambda ci: (layer, 0, 0)),
        ],
        out_specs=[
            pl.BlockSpec((batch, CHUNK, D_MODEL), lambda ci: (0, ci, 0)),
            pl.BlockSpec((batch, RET_HEADS, RET_DK, RET_DV), const4),
            pl.BlockSpec((batch, WINDOW, 128), const3),
            pl.BlockSpec((batch, WINDOW, 128), const3),
            pl.BlockSpec((batch, CONV_WIDTH - 1, CONV_CH), const3),
        ],
        out_shape=[
            jax.ShapeDtypeStruct((batch, seq, D_MODEL), BF16),
            jax.ShapeDtypeStruct((batch, RET_HEADS, RET_DK, RET_DV), F32),
            jax.ShapeDtypeStruct((batch, WINDOW, 128), F32),
            jax.ShapeDtypeStruct((batch, WINDOW, 128), F32),
            jax.ShapeDtypeStruct((batch, CONV_WIDTH - 1, CONV_CH), F32),
        ],
        scratch_shapes=[
            pltpu.VMEM((batch, RET_HEADS, RET_DK, RET_DV), F32),
            pltpu.VMEM((batch, CHUNK, 128), BF16),
            pltpu.VMEM((batch, CHUNK, 128), BF16),
            pltpu.VMEM((batch, CONV_WIDTH - 1, CONV_CH), F32),
            pltpu.VMEM((batch, CHUNK + 8, CONV_CH), F32),
        ],
        compiler_params=_params("arbitrary"),
        name="mix_prompt",
    )(attn_sinks, *([proj] * batch), *tables, conv_w)


def _mix_sample_kernel(sink_ref, proj_ref, dtab_ref, qtab_ref, ktab_ref, bias_ref, convw_ref,
                       sin_ref, kbuf_ref, vbuf_ref, cbuf_ref,
                       mixed_ref, sret_ref, knew_ref, vnew_ref, cnew_ref,
                       ubuf, kvbuf, *, layer, sdec, t_valid, seqs_per_step):
    sinks = [sink_ref[layer, hh] for hh in range(ATT_Q_HEADS)]
    for s in range(seqs_per_step):
        rows = slice(s * SEQ_PAD, (s + 1) * SEQ_PAD)
        proj = functools.partial(lambda r, lo, hi: proj_ref[r, lo:hi], rows)
        state = [sin_ref[s, h] for h in range(RET_HEADS)]
        new_state, k_cur, v_cur = _mixer_math(
            proj, state, kbuf_ref[s].astype(BF16), vbuf_ref[s].astype(BF16), cbuf_ref[s],
            (dtab_ref, qtab_ref, ktab_ref, bias_ref), 0, sdec, convw_ref[0], sinks,
            mixed_ref.at[rows], ubuf.at[s])
        for h in range(RET_HEADS):
            sret_ref[s, h] = new_state[h]
        for j, (src_ref, off, dst_ref) in enumerate(((kbuf_ref, OFF_AK, knew_ref), (vbuf_ref, OFF_AV, vnew_ref))):
            kvbuf[s, j, 0:WINDOW, :] = src_ref[s]
            kvbuf[s, j, WINDOW:WINDOW + SEQ_PAD, :] = proj(off, off + 128).astype(F32)
            dst_ref[s] = kvbuf[s, j, t_valid:t_valid + WINDOW, :]
        cnew_ref[s] = ubuf[s, 8 + t_valid - 2:8 + t_valid, :]


def _mix_sample_call(layer, proj, row0, conv_w, attn_sinks, tables, sdec, s_in, kbuf, vbuf, cbuf, t_valid):
    nb = s_in.shape[0]
    sps = 8
    assert nb % sps == 0 and row0 % (sps * SEQ_PAD) == 0
    blk0 = row0 // (sps * SEQ_PAD)
    const3 = lambda bi: (0, 0, 0)
    const4 = lambda bi: (0, 0, 0, 0)
    step3 = lambda bi: (bi, 0, 0)
    step4 = lambda bi: (bi, 0, 0, 0)
    return pl.pallas_call(
        functools.partial(_mix_sample_kernel, layer=layer, sdec=sdec, t_valid=t_valid, seqs_per_step=sps),
        grid=(nb // sps,),
        in_specs=[
            pl.BlockSpec(memory_space=pltpu.SMEM),
            pl.BlockSpec((sps * SEQ_PAD, PROJ_WIDTH), lambda bi: (blk0 + bi, 0)),
            pl.BlockSpec((RET_HEADS, SEQ_PAD, CHUNK), const3),
            pl.BlockSpec((RET_HEADS, SEQ_PAD, CHUNK), const3),
            pl.BlockSpec((RET_HEADS, CHUNK, CHUNK), const3),
            pl.BlockSpec((1, ATT_Q_HEADS, SEQ_PAD, 2 * CHUNK), const4),
            pl.BlockSpec((1, CONV_WIDTH, CONV_CH), lambda bi: (layer, 0, 0)),
            pl.BlockSpec((sps, RET_HEADS, RET_DK, RET_DV), step4),
            pl.BlockSpec((sps, WINDOW, 128), step3),
            pl.BlockSpec((sps, WINDOW, 128), step3),
            pl.BlockSpec((sps, CONV_WIDTH - 1, CONV_CH), step3),
        ],
        out_specs=[
            pl.BlockSpec((sps * SEQ_PAD, D_MODEL), lambda bi: (bi, 0)),
            pl.BlockSpec((sps, RET_HEADS, RET_DK, RET_DV), step4),
            pl.BlockSpec((sps, WINDOW, 128), step3),
            pl.BlockSpec((sps, WINDOW, 128), step3),
            pl.BlockSpec((sps, CONV_WIDTH - 1, CONV_CH), step3),
        ],
        out_shape=[
            jax.ShapeDtypeStruct((nb * SEQ_PAD, D_MODEL), BF16),
            jax.ShapeDtypeStruct((nb, RET_HEADS, RET_DK, RET_DV), F32),
            jax.ShapeDtypeStruct((nb, WINDOW, 128), F32),
            jax.ShapeDtypeStruct((nb, WINDOW, 128), F32),
            jax.ShapeDtypeStruct((nb, CONV_WIDTH - 1, CONV_CH), F32),
        ],
        scratch_shapes=[
            pltpu.VMEM((sps, SEQ_PAD + 8, CONV_CH), F32),
            pltpu.VMEM((sps, 2, WINDOW + SEQ_PAD, 128), F32),
        ],
        compiler_params=_params("arbitrary"),
        name="mix_sample",
    )(attn_sinks, proj, *tables, conv_w, s_in, kbuf, vbuf, cbuf)


def kernel(x_prompt, x_sample, state_ret, cache_win_k, cache_win_v, state_conv, c_prompt, c_sample,
           w_in, w_out, conv_w, attn_sinks, norm_g, w_ada, b_ada, w_ff_gate, w_ff_up, w_ff_down):
    batch, seq, _ = x_prompt.shape
    nb, t_valid, _ = x_sample.shape
    w_buf = cache_win_k.shape[2]
    assert w_buf == WINDOW and seq % CHUNK == 0 and t_valid <= SEQ_PAD and nb + batch <= MOD_ROWS
    rows_p, rows_s = batch * seq, nb * SEQ_PAD
    rows = rows_p + rows_s

    w_out_b, w_down_b = w_out.astype(BF16), w_ff_down.astype(BF16)

    c_all = jnp.concatenate([c_sample, c_prompt, jnp.zeros((MOD_ROWS - nb - batch, D_MODEL), F32)], axis=0)
    mod4 = _ada_call(c_all, w_ada, b_ada).reshape(DEPTH, MOD_ROWS, 1, N_MOD * D_MODEL)
    gains = norm_g.reshape(DEPTH * 4, 1, D_MODEL)

    tiles_out = tiles_down = _RowTiles(256, rows_p, rows_s, seq, nb)
    tm_stream = rows // 8
    assert rows % 8 == 0 and tm_stream % SEQ_PAD == 0
    tabs_p, sdec_p = _mixer_tables(CHUNK, CHUNK, True)
    tabs_s, sdec_s = _mixer_tables(SEQ_PAD, t_valid, False)

    xp = x_prompt.reshape(rows_p, D_MODEL)
    xs = jnp.pad(x_sample, ((0, 0), (0, SEQ_PAD - t_valid), (0, 0))).reshape(rows_s, D_MODEL)
    kbuf = cache_win_k.reshape(DEPTH, nb, w_buf, 128)
    vbuf = cache_win_v.reshape(DEPTH, nb, w_buf, 128)

    h = _prenorm_call(tiles_out, 0, xp, xs, gains, mod4)
    outs_p, outs_s = [], []
    for l in range(DEPTH):
        proj = _inproj_call(l, h, w_in, tm_stream, 1280)
        mixed_p, *st_p = _mix_prompt_call(l, proj, batch, seq, conv_w, attn_sinks, tabs_p, sdec_p)
        mixed_s, *st_s = _mix_sample_call(l, proj, rows_p, conv_w, attn_sinks, tabs_s, sdec_s,
                                          state_ret[l], kbuf[l], vbuf[l], state_conv[l], t_valid)
        mixed = (mixed_p.reshape(rows_p, D_MODEL), mixed_s)
        xp, xs, h2 = _resid_call("out_proj", tiles_out, 128, mixed, w_out_b, xp, xs, gains, mod4, l,
                                 1, MOD_G1, (l, 2, MOD_SC2, MOD_SH2))
        hid = _gateup_call(l, h2, w_ff_gate, w_ff_up, tm_stream, 512)
        next_h = (l + 1, 0, MOD_SC1, MOD_SH1) if l + 1 < DEPTH else None
        res = _resid_call("ffn_down", tiles_down, 128, hid, w_down_b, xp, xs, gains, mod4, l,
                          3, MOD_G2, next_h)
        xp, xs = res[0], res[1]
        if next_h is not None:
            h = res[2]
        outs_p.append(st_p)
        outs_s.append(st_s)

    def stack(outs, idx, shape):
        return jnp.stack([o[idx] for o in outs]).reshape(shape)

    kv_shape_p = (DEPTH, batch, w_buf, ATT_KV_HEADS, ATT_HEAD_DIM)
    kv_shape_s = (DEPTH, nb, w_buf, ATT_KV_HEADS, ATT_HEAD_DIM)
    return (
        xp.reshape(batch, seq, D_MODEL),
        xs.reshape(nb, SEQ_PAD, D_MODEL)[:, :t_valid],
        stack(outs_p, 0, (DEPTH, batch, RET_HEADS, RET_DK, RET_DV)),
        stack(outs_s, 0, (DEPTH, nb, RET_HEADS, RET_DK, RET_DV)),
        stack(outs_p, 1, kv_shape_p),
        stack(outs_s, 1, kv_shape_s),
        stack(outs_p, 2, kv_shape_p),
        stack(outs_s, 2, kv_shape_s),
        stack(outs_p, 3, (DEPTH, batch, CONV_WIDTH - 1, CONV_CH)),
        stack(outs_s, 3, (DEPTH, nb, CONV_WIDTH - 1, CONV_CH)),
    )
```

```python
import functools
import math

import numpy as np
import jax
import jax.numpy as jnp
from jax import lax
from jax.experimental import pallas as pl
from jax.experimental.pallas import tpu as pltpu

F32 = jnp.float32
BF16 = jnp.bfloat16

D_MODEL = 2048
DEPTH = 4
RET_DK = 128
RET_DV = 128
RET_HEADS = 8
CHUNK = 128
ATT_HEAD_DIM = 64
ATT_Q_HEADS = 8
ATT_KV_HEADS = 2
ATT_GROUP = 4
WINDOW = 128
CONV_WIDTH = 3
CONV_CH = 512
D_FF = 5632
N_MOD = 6
EPS = 1e-6
NEG_INF = -1e30
PROJ_WIDTH = 6400
OFF_RQ, OFF_RK, OFF_RV, OFF_RG = 0, 1024, 2048, 3072
OFF_AQ, OFF_AK, OFF_AV = 4096, 4608, 4736
OFF_CB, OFF_CC, OFF_CH = 4864, 5376, 5888
MOD_SH1, MOD_SC1, MOD_G1, MOD_SH2, MOD_SC2, MOD_G2 = range(6)

SEQ_PAD = 16
MOD_ROWS = 48
VMEM_LIMIT_V7X = 56 * 1024 * 1024

LOG_GAMMA = [math.log1p(-(2.0 ** (-5.0 - h))) for h in range(RET_HEADS)]
ALIBI_SLOPES = [2.0 ** (-8.0 * (h + 1) / ATT_Q_HEADS) for h in range(ATT_Q_HEADS)]


def _silu(a):
    return a / (1.0 + jnp.exp(-a))


def _rms(x, gain):
    return x * lax.rsqrt(jnp.mean(x * x, axis=-1, keepdims=True) + EPS) * gain


def _norm_mod(x, gain, scale, shift):
    return _rms(x, gain) * (1.0 + scale) + shift


def _params(*sem):
    return pltpu.CompilerParams(dimension_semantics=sem, vmem_limit_bytes=VMEM_LIMIT_V7X)


def _ada_kernel(c_ref, w_ref, b_ref, o_ref):
    a = _silu(c_ref[...]).astype(BF16)
    o_ref[0] = jnp.dot(a, w_ref[0].astype(BF16), preferred_element_type=F32) + b_ref[0]


def _ada_call(c_all, w_ada, b_ada):
    tn = 1024
    n = N_MOD * D_MODEL
    return pl.pallas_call(
        _ada_kernel,
        grid=(DEPTH, n // tn),
        in_specs=[
            pl.BlockSpec((MOD_ROWS, D_MODEL), lambda l, j: (0, 0)),
            pl.BlockSpec((1, D_MODEL, tn), lambda l, j: (l, 0, j)),
            pl.BlockSpec((1, 1, tn), lambda l, j: (l, 0, j)),
        ],
        out_specs=pl.BlockSpec((1, MOD_ROWS, tn), lambda l, j: (l, 0, j)),
        out_shape=jax.ShapeDtypeStruct((DEPTH, MOD_ROWS, n), F32),
        compiler_params=_params("arbitrary", "arbitrary"),
        name="ada_mod",
    )(c_all, w_ada, b_ada.reshape(DEPTH, 1, n))


class _RowTiles:
    def __init__(self, tm, rows_prompt, rows_sample, seq_len, n_seq):
        assert rows_prompt % tm == 0 and rows_sample % tm == 0 and seq_len % tm == 0 and tm % SEQ_PAD == 0
        self.tm = tm
        self.n_prompt = rows_prompt // tm
        self.n_sample = rows_sample // tm
        self.n_tiles = self.n_prompt + self.n_sample
        self.tiles_per_seq = seq_len // tm
        self.seqs_per_tile = tm // SEQ_PAD
        self.prompt_mod_row0 = n_seq

    def all_rows(self, width):
        return pl.BlockSpec((self.tm, width), lambda i: (i, 0))

    def prompt_rows(self, width):
        last = self.n_prompt - 1
        return pl.BlockSpec((self.tm, width), lambda i: (jnp.minimum(i, last), 0))

    def sample_rows(self, width):
        first = self.n_prompt
        return pl.BlockSpec((self.tm, width), lambda i: (jnp.maximum(i - first, 0), 0))

    def prompt_mod(self, layer, chunk):
        last, per, row0 = self.n_prompt - 1, self.tiles_per_seq, self.prompt_mod_row0
        return pl.BlockSpec((1, 1, 1, D_MODEL), lambda i: (layer, row0 + jnp.minimum(i, last) // per, 0, chunk))

    def sample_mod(self, layer, chunk):
        first = self.n_prompt
        return pl.BlockSpec((1, self.seqs_per_tile, 1, D_MODEL),
                            lambda i: (layer, jnp.maximum(i - first, 0), 0, chunk))


def _gain_spec(layer, which):
    return pl.BlockSpec((1, 1, D_MODEL), lambda i: (layer * 4 + which, 0, 0))


def _const_weight_spec(layer, k, n):
    return pl.BlockSpec((1, k, n), lambda i: (layer, 0, 0), pipeline_mode=pl.Buffered(1))


def _for_each_group(n_prompt_tiles, fn):
    i = pl.program_id(0)

    @pl.when(i < n_prompt_tiles)
    def _():
        fn(False)

    @pl.when(i >= n_prompt_tiles)
    def _():
        fn(True)


def _sub_blocks(tm, sub, is_sample):
    for r in range(tm // sub):
        rows = slice(r * sub, (r + 1) * sub)
        if is_sample:
            g = sub // SEQ_PAD
            yield rows, (g, SEQ_PAD, D_MODEL), slice(r * g, (r + 1) * g)
        else:
            yield rows, (1, sub, D_MODEL), slice(0, 1)


def _prenorm_kernel(xp_ref, xs_ref, gain_ref, scp_ref, shp_ref, scs_ref, shs_ref, h_ref, *, tiles, sub):
    def run(is_sample):
        x_ref, sc_ref, sh_ref = (xs_ref, scs_ref, shs_ref) if is_sample else (xp_ref, scp_ref, shp_ref)
        for rows, shape3, seqs in _sub_blocks(tiles.tm, sub, is_sample):
            h = _norm_mod(x_ref[rows, :].reshape(shape3), gain_ref[...], sc_ref[0, seqs], sh_ref[0, seqs])
            h_ref[rows, :] = h.reshape(sub, D_MODEL).astype(BF16)

    _for_each_group(tiles.n_prompt, run)


def _prenorm_call(tiles, layer, xp, xs, gains, mod4):
    return pl.pallas_call(
        functools.partial(_prenorm_kernel, tiles=tiles, sub=128),
        grid=(tiles.n_tiles,),
        in_specs=[
            tiles.prompt_rows(D_MODEL), tiles.sample_rows(D_MODEL), _gain_spec(layer, 0),
            tiles.prompt_mod(layer, MOD_SC1), tiles.prompt_mod(layer, MOD_SH1),
            tiles.sample_mod(layer, MOD_SC1), tiles.sample_mod(layer, MOD_SH1),
        ],
        out_specs=tiles.all_rows(D_MODEL),
        out_shape=jax.ShapeDtypeStruct((tiles.n_tiles * tiles.tm, D_MODEL), BF16),
        compiler_params=_params("arbitrary"),
        name="prenorm",
    )(xp, xs, gains, mod4, mod4, mod4, mod4)


def _inproj_kernel(h_ref, w_ref, o_ref, wb_scr):
    @pl.when(pl.program_id(1) == 0)
    def _():
        wb_scr[...] = w_ref[0].astype(BF16)

    o_ref[...] = jnp.dot(h_ref[...], wb_scr[...], preferred_element_type=F32).astype(o_ref.dtype)


def _inproj_call(layer, h, w_in, tm, tn):
    rows = h.shape[0]
    return pl.pallas_call(
        _inproj_kernel,
        grid=(PROJ_WIDTH // tn, rows // tm),
        in_specs=[
            pl.BlockSpec((tm, D_MODEL), lambda j, i: (i, 0)),
            pl.BlockSpec((1, D_MODEL, tn), lambda j, i: (layer, 0, j)),
        ],
        out_specs=pl.BlockSpec((tm, tn), lambda j, i: (i, j)),
        out_shape=jax.ShapeDtypeStruct((rows, PROJ_WIDTH), BF16),
        scratch_shapes=[pltpu.VMEM((D_MODEL, tn), BF16)],
        compiler_params=_params("arbitrary", "arbitrary"),
        name="in_proj",
    )(h, w_in)


def _gateup_kernel(h_ref, wg_ref, wu_ref, wd_ref, o_ref, wdb_ref, wg_scr, wu_scr):
    @pl.when(pl.program_id(1) == 0)
    def _():
        wg_scr[...] = wg_ref[0].astype(BF16)
        wu_scr[...] = wu_ref[0].astype(BF16)
        wdb_ref[0] = wd_ref[0].astype(BF16)

    h = h_ref[...]
    a = jnp.dot(h, wg_scr[...], preferred_element_type=F32)
    b = jnp.dot(h, wu_scr[...], preferred_element_type=F32)
    o_ref[...] = (_silu(a) * b).astype(o_ref.dtype)


def _gateup_call(layer, h2, w_gate, w_up, w_down, tm, tn):
    rows = h2.shape[0]
    w_spec = pl.BlockSpec((1, D_MODEL, tn), lambda j, i: (layer, 0, j))
    return pl.pallas_call(
        _gateup_kernel,
        grid=(D_FF // tn, rows // tm),
        in_specs=[pl.BlockSpec((tm, D_MODEL), lambda j, i: (i, 0)), w_spec, w_spec,
                  pl.BlockSpec((1, tn, D_MODEL), lambda j, i: (layer, j, 0))],
        out_specs=[pl.BlockSpec((tm, tn), lambda j, i: (i, j)),
                   pl.BlockSpec((1, tn, D_MODEL), lambda j, i: (0, j, 0))],
        out_shape=[jax.ShapeDtypeStruct((rows, D_FF), BF16), jax.ShapeDtypeStruct((1, D_FF, D_MODEL), BF16)],
        scratch_shapes=[pltpu.VMEM((D_MODEL, tn), BF16), pltpu.VMEM((D_MODEL, tn), BF16)],
        compiler_params=_params("arbitrary", "arbitrary"),
        name="gate_up",
    )(h2, w_gate, w_up, w_down)


def _resid_kernel(*refs, tiles, sub, emit_h, a_split):
    refs = list(refs)
    ap_ref = refs.pop(0)
    as_ref = refs.pop(0) if a_split else ap_ref
    if emit_h:
        (w_ref, xp_ref, xs_ref, gain1_ref, gp_ref, gs_ref, gain2_ref, scp_ref, shp_ref, scs_ref, shs_ref,
         yp_ref, ys_ref, h_ref) = refs
    else:
        w_ref, xp_ref, xs_ref, gain1_ref, gp_ref, gs_ref, yp_ref, ys_ref = refs

    def run(is_sample):
        a_ref, x_ref, y_ref, g_ref = (as_ref, xs_ref, ys_ref, gs_ref) if is_sample else (ap_ref, xp_ref, yp_ref, gp_ref)
        for rows, shape3, seqs in _sub_blocks(tiles.tm, sub, is_sample):
            o = jnp.dot(a_ref[rows, :], w_ref[0], preferred_element_type=F32).reshape(shape3)
            gated_gain = gain1_ref[...] * g_ref[0, seqs]
            y = x_ref[rows, :].reshape(shape3) + o * lax.rsqrt(jnp.mean(o * o, axis=-1, keepdims=True) + EPS) * gated_gain
            y_ref[rows, :] = y.reshape(sub, D_MODEL)
            if emit_h:
                sc_ref, sh_ref = (scs_ref, shs_ref) if is_sample else (scp_ref, shp_ref)
                scaled_gain = gain2_ref[...] * (1.0 + sc_ref[0, seqs])
                h = y * lax.rsqrt(jnp.mean(y * y, axis=-1, keepdims=True) + EPS) * scaled_gain + sh_ref[0, seqs]
                h_ref[rows, :] = h.reshape(sub, D_MODEL).astype(BF16)

    _for_each_group(tiles.n_prompt, run)


def _resid_call(name, tiles, sub, a, w_b, xp, xs, gains, mod4, layer, gain1, gate_chunk, h_params):
    emit_h = h_params is not None
    a_split = isinstance(a, tuple)
    k_dim = w_b.shape[1]
    if a_split:
        in_specs, args = [tiles.prompt_rows(k_dim), tiles.sample_rows(k_dim)], list(a)
    else:
        in_specs, args = [tiles.all_rows(k_dim)], [a]
    in_specs += [
        _const_weight_spec(0, k_dim, D_MODEL),
        tiles.prompt_rows(D_MODEL), tiles.sample_rows(D_MODEL),
        _gain_spec(layer, gain1), tiles.prompt_mod(layer, gate_chunk), tiles.sample_mod(layer, gate_chunk),
    ]
    args += [w_b, xp, xs, gains, mod4, mod4]
    out_specs = [tiles.prompt_rows(D_MODEL), tiles.sample_rows(D_MODEL)]
    out_shape = [jax.ShapeDtypeStruct(xp.shape, F32), jax.ShapeDtypeStruct(xs.shape, F32)]
    if emit_h:
        hl, hg, hsc, hsh = h_params
        in_specs += [_gain_spec(hl, hg), tiles.prompt_mod(hl, hsc), tiles.prompt_mod(hl, hsh),
                     tiles.sample_mod(hl, hsc), tiles.sample_mod(hl, hsh)]
        args += [gains, mod4, mod4, mod4, mod4]
        out_specs.append(tiles.all_rows(D_MODEL))
        out_shape.append(jax.ShapeDtypeStruct((tiles.n_tiles * tiles.tm, D_MODEL), BF16))
    return pl.pallas_call(
        functools.partial(_resid_kernel, tiles=tiles, sub=sub, emit_h=emit_h, a_split=a_split),
        grid=(tiles.n_tiles,),
        in_specs=in_specs,
        out_specs=out_specs,
        out_shape=out_shape,
        compiler_params=_params("arbitrary"),
        name=name,
    )(*args)


def _mixer_tables(t_rows, t_valid, first_chunk_variant):
    qi = np.arange(t_rows)[:, None]
    kj = np.arange(2 * CHUNK)[None, :]
    dist = qi + CHUNK - kj
    allowed = (dist >= 0) & (dist < WINDOW)
    variants = [allowed & (kj >= CHUNK), allowed] if first_chunk_variant else [allowed]
    bias = np.stack([np.stack([np.where(ok, -slope * dist.astype(np.float64), NEG_INF) for slope in ALIBI_SLOPES])
                     for ok in variants])
    scale = RET_DK ** -0.5
    i = np.arange(t_rows)[:, None].astype(np.float64)
    j = np.arange(CHUNK)[None, :].astype(np.float64)
    jr = np.arange(CHUNK)[:, None].astype(np.float64)
    dtab = np.zeros((RET_HEADS, t_rows, CHUNK), np.float64)
    qtab = np.zeros((RET_HEADS, t_rows, CHUNK), np.float64)
    ktab = np.zeros((RET_HEADS, CHUNK, CHUNK), np.float64)
    sdec = []
    for h, lg in enumerate(LOG_GAMMA):
        causal = (i >= j) & (i < t_valid) & (j < t_valid)
        dtab[h] = np.where(causal, scale * np.exp(np.maximum(i - j, 0.0) * lg), 0.0)
        qtab[h] = np.where(i < t_valid, np.exp((i + 1.0) * lg), 0.0) * np.ones((1, CHUNK))
        ktab[h] = np.where(jr < t_valid, scale * np.exp(np.maximum(t_valid - 1.0 - jr, 0.0) * lg), 0.0) * np.ones((1, CHUNK))
        sdec.append(float(np.exp(t_valid * lg)))
    return tuple(jnp.asarray(t, F32) for t in (dtab, qtab, ktab, bias)), sdec


class _Chain:
    def __init__(self, proj, state, k_prev, v_prev, u_prev, mixed_ref, ubuf):
        self.proj, self.state, self.k_prev, self.v_prev, self.u_prev = proj, state, k_prev, v_prev, u_prev
        self.mixed_ref, self.ubuf = mixed_ref, ubuf


def _mixer_math(chains, tabs, bias_variant, sdec, conv_w, sinks):
    dtab_ref, qtab_ref, ktab_ref, bias_ref = tabs
    t_rows = chains[0].mixed_ref.shape[0]
    pad = CHUNK - t_rows
    nt = (((1,), (1,)), ((), ()))
    tn = (((0,), (0,)), ((), ()))
    att_base = RET_HEADS * RET_DV
    conv_base = att_base + ATT_Q_HEADS * ATT_HEAD_DIM
    att_scale = ATT_HEAD_DIM ** -0.5
    kv_lanes = [slice(g * ATT_HEAD_DIM, (g + 1) * ATT_HEAD_DIM) for g in range(ATT_KV_HEADS)]
    heads, att_heads = range(RET_HEADS), range(ATT_Q_HEADS)

    def pad_rows(a):
        if pad == 0:
            return a
        return jnp.concatenate([a, jnp.zeros((pad, a.shape[1]), a.dtype)], axis=0)

    for ch in chains:
        proj = ch.proj
        ch.k_cur = pad_rows(proj(OFF_AK, OFF_AK + 128))
        ch.v_cur = pad_rows(proj(OFF_AV, OFF_AV + 128))
        k_all = jnp.concatenate([ch.k_prev, ch.k_cur], axis=0)
        ch.v_all = jnp.concatenate([ch.v_prev, ch.v_cur], axis=0)
        rq = [proj(OFF_RQ + h * RET_DK, OFF_RQ + (h + 1) * RET_DK) for h in heads]
        rk = [pad_rows(proj(OFF_RK + h * RET_DK, OFF_RK + (h + 1) * RET_DK)) for h in heads]
        ch.rv = [pad_rows(proj(OFF_RV + h * RET_DV, OFF_RV + (h + 1) * RET_DV)) for h in heads]
        ch.ret_s = [lax.dot_general(rq[h], rk[h], nt, preferred_element_type=F32) for h in heads]
        ch.inter = [jnp.dot(rq[h], ch.state[h].astype(BF16), preferred_element_type=F32) for h in heads]
        ch.att_s = []
        for hh in att_heads:
            q = proj(OFF_AQ + hh * ATT_HEAD_DIM, OFF_AQ + (hh + 1) * ATT_HEAD_DIM) * att_scale
            ch.att_s.append(lax.dot_general(q, k_all[:, kv_lanes[hh // ATT_GROUP]], nt, preferred_element_type=F32))
        ch.new_state = []
        for h in heads:
            ks = (rk[h].astype(F32) * ktab_ref[h]).astype(BF16)
            kv = lax.dot_general(ks, ch.rv[h], tn, preferred_element_type=F32)
            ch.new_state.append(sdec[h] * ch.state[h] + kv)

    for ch in chains:
        ch.ret_p = [(ch.ret_s[h] * dtab_ref[h]).astype(BF16) for h in heads]
        ch.att_e, ch.att_den = [], []
        for hh in att_heads:
            s = ch.att_s[hh] + bias_ref[bias_variant, hh]
            sink = sinks[hh]
            m = jnp.maximum(jnp.max(s, axis=-1, keepdims=True), sink)
            e = jnp.exp(s - m)
            ch.att_den.append(jnp.sum(e, axis=-1, keepdims=True) + jnp.exp(sink - m))
            ch.att_e.append(e.astype(BF16))
        proj, ubuf = ch.proj, ch.ubuf
        gate_b = proj(OFF_CB, OFF_CB + CONV_CH).astype(F32)
        u = proj(OFF_CC, OFF_CC + CONV_CH).astype(F32) * proj(OFF_CH, OFF_CH + CONV_CH).astype(F32)
        ubuf[6:8, :] = ch.u_prev
        ubuf[8:8 + t_rows, :] = u
        y = conv_w[0:1, :] * ubuf[6:6 + t_rows, :] + conv_w[1:2, :] * ubuf[7:7 + t_rows, :] + conv_w[2:3, :] * u
        ch.mixed_ref[:, conv_base:conv_base + CONV_CH] = (gate_b * y).astype(ch.mixed_ref.dtype)

    for ch in chains:
        ch.intra = [jnp.dot(ch.ret_p[h], ch.rv[h], preferred_element_type=F32) for h in heads]
        ch.att_o = [jnp.dot(ch.att_e[hh], ch.v_all[:, kv_lanes[hh // ATT_GROUP]], preferred_element_type=F32)
                    for hh in att_heads]

    for ch in chains:
        for h in heads:
            gate = ch.proj(OFF_RG + h * RET_DV, OFF_RG + (h + 1) * RET_DV).astype(F32)
            ry = ch.intra[h] + ch.inter[h] * qtab_ref[h]
            ryn = ry * lax.rsqrt(jnp.mean(ry * ry, axis=-1, keepdims=True) + EPS)
            ch.mixed_ref[:, h * RET_DV:(h + 1) * RET_DV] = (_silu(gate) * ryn).astype(ch.mixed_ref.dtype)
        for hh in att_heads:
            c0 = att_base + hh * ATT_HEAD_DIM
            ch.mixed_ref[:, c0:c0 + ATT_HEAD_DIM] = (ch.att_o[hh] / ch.att_den[hh]).astype(ch.mixed_ref.dtype)
    return [(ch.new_state, ch.k_cur, ch.v_cur) for ch in chains]


def _mix_prompt_kernel(sink_ref, *refs, layer, sdec, batch):
    proj_refs, refs = refs[:batch], refs[batch:]
    (dtab_ref, qtab_ref, ktab_ref, bias_ref, convw_ref, wout_ref, mixed_ref, sret_ref, knew_ref, vnew_ref, cnew_ref,
     woutb_ref, s_scr, kprev_scr, vprev_scr, uprev_scr, ubuf) = refs
    c = pl.program_id(0)
    woutb_ref[0] = wout_ref[0].astype(BF16)

    @pl.when(c == 0)
    def _():
        s_scr[...] = jnp.zeros_like(s_scr)
        kprev_scr[...] = jnp.zeros_like(kprev_scr)
        vprev_scr[...] = jnp.zeros_like(vprev_scr)
        uprev_scr[...] = jnp.zeros_like(uprev_scr)

    sinks = [sink_ref[layer, hh] for hh in range(ATT_Q_HEADS)]
    bias_variant = jnp.where(c == 0, 0, 1)
    last = c == pl.num_programs(0) - 1
    chains = [_Chain(functools.partial(lambda ref, lo, hi: ref[:, lo:hi], proj_refs[b]),
                     [s_scr[b, h] for h in range(RET_HEADS)], kprev_scr[b], vprev_scr[b], uprev_scr[b],
                     mixed_ref.at[b], ubuf.at[b]) for b in range(batch)]
    results = [_mixer_math([ch], (dtab_ref, qtab_ref, ktab_ref, bias_ref), bias_variant, sdec, convw_ref[0], sinks)[0]
               for ch in chains]
    for b, (new_state, k_cur, v_cur) in enumerate(results):
        for h in range(RET_HEADS):
            s_scr[b, h] = new_state[h]
        kprev_scr[b] = k_cur
        vprev_scr[b] = v_cur
        uprev_scr[b] = ubuf[b, CHUNK + 6:CHUNK + 8, :]

        @pl.when(last)
        def _():
            for h in range(RET_HEADS):
                sret_ref[b, h] = new_state[h]
            knew_ref[b] = k_cur.astype(F32)
            vnew_ref[b] = v_cur.astype(F32)
            cnew_ref[b] = ubuf[b, CHUNK + 6:CHUNK + 8, :]


def _mix_prompt_call(layer, proj, batch, seq, conv_w, attn_sinks, tables, sdec, w_out):
    n_chunks = seq // CHUNK
    wrows = D_MODEL // n_chunks
    assert D_MODEL % n_chunks == 0 and wrows % 16 == 0
    const3 = lambda ci: (0, 0, 0)
    const4 = lambda ci: (0, 0, 0, 0)
    proj_specs = [pl.BlockSpec((CHUNK, PROJ_WIDTH), functools.partial(lambda b, ci: (b * n_chunks + ci, 0), b))
                  for b in range(batch)]
    return pl.pallas_call(
        functools.partial(_mix_prompt_kernel, layer=layer, sdec=sdec, batch=batch),
        grid=(n_chunks,),
        in_specs=[pl.BlockSpec(memory_space=pltpu.SMEM)] + proj_specs + [
            pl.BlockSpec((RET_HEADS, CHUNK, CHUNK), const3),
            pl.BlockSpec((RET_HEADS, CHUNK, CHUNK), const3),
            pl.BlockSpec((RET_HEADS, CHUNK, CHUNK), const3),
            pl.BlockSpec((2, ATT_Q_HEADS, CHUNK, 2 * CHUNK), const4),
            pl.BlockSpec((1, CONV_WIDTH, CONV_CH), lambda ci: (layer, 0, 0)),
            pl.BlockSpec((1, wrows, D_MODEL), lambda ci: (layer, ci, 0)),
        ],
        out_specs=[
            pl.BlockSpec((batch, CHUNK, D_MODEL), lambda ci: (0, ci, 0)),
            pl.BlockSpec((batch, RET_HEADS, RET_DK, RET_DV), const4),
            pl.BlockSpec((batch, WINDOW, 128), const3),
            pl.BlockSpec((batch, WINDOW, 128), const3),
            pl.BlockSpec((batch, CONV_WIDTH - 1, CONV_CH), const3),
            pl.BlockSpec((1, wrows, D_MODEL), lambda ci: (0, ci, 0)),
        ],
        out_shape=[
            jax.ShapeDtypeStruct((batch, seq, D_MODEL), BF16),
            jax.ShapeDtypeStruct((batch, RET_HEADS, RET_DK, RET_DV), F32),
            jax.ShapeDtypeStruct((batch, WINDOW, 128), F32),
            jax.ShapeDtypeStruct((batch, WINDOW, 128), F32),
            jax.ShapeDtypeStruct((batch, CONV_WIDTH - 1, CONV_CH), F32),
            jax.ShapeDtypeStruct((1, D_MODEL, D_MODEL), BF16),
        ],
        scratch_shapes=[
            pltpu.VMEM((batch, RET_HEADS, RET_DK, RET_DV), F32),
            pltpu.VMEM((batch, CHUNK, 128), BF16),
            pltpu.VMEM((batch, CHUNK, 128), BF16),
            pltpu.VMEM((batch, CONV_WIDTH - 1, CONV_CH), F32),
            pltpu.VMEM((batch, CHUNK + 8, CONV_CH), F32),
        ],
        compiler_params=_params("arbitrary"),
        name="mix_prompt",
    )(attn_sinks, *([proj] * batch), *tables, conv_w, w_out)


def _mix_sample_kernel(sink_ref, proj_ref, dtab_ref, qtab_ref, ktab_ref, bias_ref, convw_ref,
                       sin_ref, kbuf_ref, vbuf_ref, cbuf_ref,
                       mixed_ref, sret_ref, knew_ref, vnew_ref, cnew_ref,
                       ubuf, kvbuf, *, layer, sdec, t_valid, seqs_per_step):
    sinks = [sink_ref[layer, hh] for hh in range(ATT_Q_HEADS)]
    chains = []
    for s in range(seqs_per_step):
        rows = slice(s * SEQ_PAD, (s + 1) * SEQ_PAD)
        chains.append(_Chain(functools.partial(lambda r, lo, hi: proj_ref[r, lo:hi], rows),
                             [sin_ref[s, h] for h in range(RET_HEADS)], kbuf_ref[s].astype(BF16),
                             vbuf_ref[s].astype(BF16), cbuf_ref[s], mixed_ref.at[rows], ubuf.at[s]))
    results = _mixer_math(chains, (dtab_ref, qtab_ref, ktab_ref, bias_ref), 0, sdec, convw_ref[0], sinks)
    for s, (new_state, _, _) in enumerate(results):
        for h in range(RET_HEADS):
            sret_ref[s, h] = new_state[h]
        for j, (src_ref, off, dst_ref) in enumerate(((kbuf_ref, OFF_AK, knew_ref), (vbuf_ref, OFF_AV, vnew_ref))):
            kvbuf[s, j, 0:WINDOW, :] = src_ref[s]
            kvbuf[s, j, WINDOW:WINDOW + SEQ_PAD, :] = chains[s].proj(off, off + 128).astype(F32)
            dst_ref[s] = kvbuf[s, j, t_valid:t_valid + WINDOW, :]
        cnew_ref[s] = ubuf[s, 8 + t_valid - 2:8 + t_valid, :]


def _mix_sample_call(layer, proj, row0, conv_w, attn_sinks, tables, sdec, s_in, kbuf, vbuf, cbuf, t_valid):
    nb = s_in.shape[0]
    sps = 8
    assert nb % sps == 0 and row0 % (sps * SEQ_PAD) == 0
    blk0 = row0 // (sps * SEQ_PAD)
    const3 = lambda bi: (0, 0, 0)
    const4 = lambda bi: (0, 0, 0, 0)
    step3 = lambda bi: (bi, 0, 0)
    step4 = lambda bi: (bi, 0, 0, 0)
    return pl.pallas_call(
        functools.partial(_mix_sample_kernel, layer=layer, sdec=sdec, t_valid=t_valid, seqs_per_step=sps),
        grid=(nb // sps,),
        in_specs=[
            pl.BlockSpec(memory_space=pltpu.SMEM),
            pl.BlockSpec((sps * SEQ_PAD, PROJ_WIDTH), lambda bi: (blk0 + bi, 0)),
            pl.BlockSpec((RET_HEADS, SEQ_PAD, CHUNK), const3),
            pl.BlockSpec((RET_HEADS, SEQ_PAD, CHUNK), const3),
            pl.BlockSpec((RET_HEADS, CHUNK, CHUNK), const3),
            pl.BlockSpec((1, ATT_Q_HEADS, SEQ_PAD, 2 * CHUNK), const4),
            pl.BlockSpec((1, CONV_WIDTH, CONV_CH), lambda bi: (layer, 0, 0)),
            pl.BlockSpec((sps, RET_HEADS, RET_DK, RET_DV), step4),
            pl.BlockSpec((sps, WINDOW, 128), step3),
            pl.BlockSpec((sps, WINDOW, 128), step3),
            pl.BlockSpec((sps, CONV_WIDTH - 1, CONV_CH), step3),
        ],
        out_specs=[
            pl.BlockSpec((sps * SEQ_PAD, D_MODEL), lambda bi: (bi, 0)),
            pl.BlockSpec((sps, RET_HEADS, RET_DK, RET_DV), step4),
            pl.BlockSpec((sps, WINDOW, 128), step3),
            pl.BlockSpec((sps, WINDOW, 128), step3),
            pl.BlockSpec((sps, CONV_WIDTH - 1, CONV_CH), step3),
        ],
        out_shape=[
            jax.ShapeDtypeStruct((nb * SEQ_PAD, D_MODEL), BF16),
            jax.ShapeDtypeStruct((nb, RET_HEADS, RET_DK, RET_DV), F32),
            jax.ShapeDtypeStruct((nb, WINDOW, 128), F32),
            jax.ShapeDtypeStruct((nb, WINDOW, 128), F32),
            jax.ShapeDtypeStruct((nb, CONV_WIDTH - 1, CONV_CH), F32),
        ],
        scratch_shapes=[
            pltpu.VMEM((sps, SEQ_PAD + 8, CONV_CH), F32),
            pltpu.VMEM((sps, 2, WINDOW + SEQ_PAD, 128), F32),
        ],
        compiler_params=_params("arbitrary"),
        name="mix_sample",
    )(attn_sinks, proj, *tables, conv_w, s_in, kbuf, vbuf, cbuf)


def kernel(x_prompt, x_sample, state_ret, cache_win_k, cache_win_v, state_conv, c_prompt, c_sample,
           w_in, w_out, conv_w, attn_sinks, norm_g, w_ada, b_ada, w_ff_gate, w_ff_up, w_ff_down):
    batch, seq, _ = x_prompt.shape
    nb, t_valid, _ = x_sample.shape
    w_buf = cache_win_k.shape[2]
    assert w_buf == WINDOW and seq % CHUNK == 0 and t_valid <= SEQ_PAD and nb + batch <= MOD_ROWS
    rows_p, rows_s = batch * seq, nb * SEQ_PAD
    rows = rows_p + rows_s

    c_all = jnp.concatenate([c_sample, c_prompt, jnp.zeros((MOD_ROWS - nb - batch, D_MODEL), F32)], axis=0)
    mod4 = _ada_call(c_all, w_ada, b_ada).reshape(DEPTH, MOD_ROWS, 1, N_MOD * D_MODEL)
    gains = norm_g.reshape(DEPTH * 4, 1, D_MODEL)

    tiles_out = tiles_down = _RowTiles(256, rows_p, rows_s, seq, nb)
    tm_stream = rows // 8
    assert rows % 8 == 0 and tm_stream % SEQ_PAD == 0
    tabs_p, sdec_p = _mixer_tables(CHUNK, CHUNK, True)
    tabs_s, sdec_s = _mixer_tables(SEQ_PAD, t_valid, False)

    xp = x_prompt.reshape(rows_p, D_MODEL)
    xs = jnp.pad(x_sample, ((0, 0), (0, SEQ_PAD - t_valid), (0, 0))).reshape(rows_s, D_MODEL)
    kbuf = cache_win_k.reshape(DEPTH, nb, w_buf, 128)
    vbuf = cache_win_v.reshape(DEPTH, nb, w_buf, 128)

    h = _prenorm_call(tiles_out, 0, xp, xs, gains, mod4)
    outs_p, outs_s = [], []
    for l in range(DEPTH):
        proj = _inproj_call(l, h, w_in, tm_stream, 1280)
        mixed_p, *st_p, w_out_b = _mix_prompt_call(l, proj, batch, seq, conv_w, attn_sinks, tabs_p, sdec_p, w_out)
        mixed_s, *st_s = _mix_sample_call(l, proj, rows_p, conv_w, attn_sinks, tabs_s, sdec_s,
                                          state_ret[l], kbuf[l], vbuf[l], state_conv[l], t_valid)
        mixed = (mixed_p.reshape(rows_p, D_MODEL), mixed_s)
        xp, xs, h2 = _resid_call("out_proj", tiles_out, 128, mixed, w_out_b, xp, xs, gains, mod4, l,
                                 1, MOD_G1, (l, 2, MOD_SC2, MOD_SH2))
        hid, w_down_b = _gateup_call(l, h2, w_ff_gate, w_ff_up, w_ff_down, tm_stream, 512)
        next_h = (l + 1, 0, MOD_SC1, MOD_SH1) if l + 1 < DEPTH else None
        res = _resid_call("ffn_down", tiles_down, 128, hid, w_down_b, xp, xs, gains, mod4, l,
                          3, MOD_G2, next_h)
        xp, xs = res[0], res[1]
        if next_h is not None:
            h = res[2]
        outs_p.append(st_p)
        outs_s.append(st_s)

    def stack(outs, idx, shape):
        return jnp.stack([o[idx] for o in outs]).reshape(shape)

    kv_shape_p = (DEPTH, batch, w_buf, ATT_KV_HEADS, ATT_HEAD_DIM)
    kv_shape_s = (DEPTH, nb, w_buf, ATT_KV_HEADS, ATT_HEAD_DIM)
    return (
        xp.reshape(batch, seq, D_MODEL),
        xs.reshape(nb, SEQ_PAD, D_MODEL)[:, :t_valid],
        stack(outs_p, 0, (DEPTH, batch, RET_HEADS, RET_DK, RET_DV)),
        stack(outs_s, 0, (DEPTH, nb, RET_HEADS, RET_DK, RET_DV)),
        stack(outs_p, 1, kv_shape_p),
        stack(outs_s, 1, kv_shape_s),
        stack(outs_p, 2, kv_shape_p),
        stack(outs_s, 2, kv_shape_s),
        stack(outs_p, 3, (DEPTH, batch, CONV_WIDTH - 1, CONV_CH)),
        stack(outs_s, 3, (DEPTH, nb, CONV_WIDTH - 1, CONV_CH)),
    )
```

```python
import functools
import math

import numpy as np
import jax
import jax.numpy as jnp
from jax import lax
from jax.experimental import pallas as pl
from jax.experimental.pallas import tpu as pltpu

F32 = jnp.float32
BF16 = jnp.bfloat16

D_MODEL = 2048
DEPTH = 4
RET_DK = 128
RET_DV = 128
RET_HEADS = 8
CHUNK = 128
ATT_HEAD_DIM = 64
ATT_Q_HEADS = 8
ATT_KV_HEADS = 2
ATT_GROUP = 4
WINDOW = 128
CONV_WIDTH = 3
CONV_CH = 512
D_FF = 5632
N_MOD = 6
EPS = 1e-6
NEG_INF = -1e30
PROJ_WIDTH = 6400
OFF_RQ, OFF_RK, OFF_RV, OFF_RG = 0, 1024, 2048, 3072
OFF_AQ, OFF_AK, OFF_AV = 4096, 4608, 4736
OFF_CB, OFF_CC, OFF_CH = 4864, 5376, 5888
MOD_SH1, MOD_SC1, MOD_G1, MOD_SH2, MOD_SC2, MOD_G2 = range(6)

SEQ_PAD = 16
MOD_ROWS = 48
VMEM_LIMIT_V7X = 56 * 1024 * 1024

LOG_GAMMA = [math.log1p(-(2.0 ** (-5.0 - h))) for h in range(RET_HEADS)]
ALIBI_SLOPES = [2.0 ** (-8.0 * (h + 1) / ATT_Q_HEADS) for h in range(ATT_Q_HEADS)]


def _silu(a):
    return a / (1.0 + jnp.exp(-a))


def _rms(x, gain):
    return x * lax.rsqrt(jnp.mean(x * x, axis=-1, keepdims=True) + EPS) * gain


def _norm_mod(x, gain, scale, shift):
    return _rms(x, gain) * (1.0 + scale) + shift


def _params(*sem):
    return pltpu.CompilerParams(dimension_semantics=sem, vmem_limit_bytes=VMEM_LIMIT_V7X)


def _ada_kernel(c_ref, w_ref, b_ref, o_ref):
    a = _silu(c_ref[...]).astype(BF16)
    o_ref[0] = jnp.dot(a, w_ref[0].astype(BF16), preferred_element_type=F32) + b_ref[0]


def _ada_call(c_all, w_ada, b_ada):
    tn = 1024
    n = N_MOD * D_MODEL
    return pl.pallas_call(
        _ada_kernel,
        grid=(DEPTH, n // tn),
        in_specs=[
            pl.BlockSpec((MOD_ROWS, D_MODEL), lambda l, j: (0, 0)),
            pl.BlockSpec((1, D_MODEL, tn), lambda l, j: (l, 0, j)),
            pl.BlockSpec((1, 1, tn), lambda l, j: (l, 0, j)),
        ],
        out_specs=pl.BlockSpec((1, MOD_ROWS, tn), lambda l, j: (l, 0, j)),
        out_shape=jax.ShapeDtypeStruct((DEPTH, MOD_ROWS, n), F32),
        compiler_params=_params("arbitrary", "arbitrary"),
        name="ada_mod",
    )(c_all, w_ada, b_ada.reshape(DEPTH, 1, n))


class _RowTiles:
    def __init__(self, tm, rows_prompt, rows_sample, seq_len, n_seq):
        assert rows_prompt % tm == 0 and rows_sample % tm == 0 and seq_len % tm == 0 and tm % SEQ_PAD == 0
        self.tm = tm
        self.n_prompt = rows_prompt // tm
        self.n_sample = rows_sample // tm
        self.n_tiles = self.n_prompt + self.n_sample
        self.tiles_per_seq = seq_len // tm
        self.seqs_per_tile = tm // SEQ_PAD
        self.prompt_mod_row0 = n_seq

    def all_rows(self, width):
        return pl.BlockSpec((self.tm, width), lambda i: (i, 0))

    def prompt_rows(self, width):
        last = self.n_prompt - 1
        return pl.BlockSpec((self.tm, width), lambda i: (jnp.minimum(i, last), 0))

    def sample_rows(self, width):
        first = self.n_prompt
        return pl.BlockSpec((self.tm, width), lambda i: (jnp.maximum(i - first, 0), 0))

    def prompt_mod(self, layer, chunk):
        last, per, row0 = self.n_prompt - 1, self.tiles_per_seq, self.prompt_mod_row0
        return pl.BlockSpec((1, 1, 1, D_MODEL), lambda i: (layer, row0 + jnp.minimum(i, last) // per, 0, chunk))

    def sample_mod(self, layer, chunk):
        first = self.n_prompt
        return pl.BlockSpec((1, self.seqs_per_tile, 1, D_MODEL),
                            lambda i: (layer, jnp.maximum(i - first, 0), 0, chunk))


def _gain_spec(layer, which):
    return pl.BlockSpec((1, 1, D_MODEL), lambda i: (layer * 4 + which, 0, 0))


def _const_weight_spec(layer, k, n):
    return pl.BlockSpec((1, k, n), lambda i: (layer, 0, 0), pipeline_mode=pl.Buffered(1))


def _for_each_group(n_prompt_tiles, fn):
    i = pl.program_id(0)

    @pl.when(i < n_prompt_tiles)
    def _():
        fn(False)

    @pl.when(i >= n_prompt_tiles)
    def _():
        fn(True)


def _sub_blocks(tm, sub, is_sample):
    for r in range(tm // sub):
        rows = slice(r * sub, (r + 1) * sub)
        if is_sample:
            g = sub // SEQ_PAD
            yield rows, (g, SEQ_PAD, D_MODEL), slice(r * g, (r + 1) * g)
        else:
            yield rows, (1, sub, D_MODEL), slice(0, 1)


def _prenorm_kernel(xp_ref, xs_ref, gain_ref, scp_ref, shp_ref, scs_ref, shs_ref, h_ref, *, tiles, sub):
    def run(is_sample):
        x_ref, sc_ref, sh_ref = (xs_ref, scs_ref, shs_ref) if is_sample else (xp_ref, scp_ref, shp_ref)
        for rows, shape3, seqs in _sub_blocks(tiles.tm, sub, is_sample):
            h = _norm_mod(x_ref[rows, :].reshape(shape3), gain_ref[...], sc_ref[0, seqs], sh_ref[0, seqs])
            h_ref[rows, :] = h.reshape(sub, D_MODEL).astype(BF16)

    _for_each_group(tiles.n_prompt, run)


def _prenorm_call(tiles, layer, xp, xs, gains, mod4):
    return pl.pallas_call(
        functools.partial(_prenorm_kernel, tiles=tiles, sub=128),
        grid=(tiles.n_tiles,),
        in_specs=[
            tiles.prompt_rows(D_MODEL), tiles.sample_rows(D_MODEL), _gain_spec(layer, 0),
            tiles.prompt_mod(layer, MOD_SC1), tiles.prompt_mod(layer, MOD_SH1),
            tiles.sample_mod(layer, MOD_SC1), tiles.sample_mod(layer, MOD_SH1),
        ],
        out_specs=tiles.all_rows(D_MODEL),
        out_shape=jax.ShapeDtypeStruct((tiles.n_tiles * tiles.tm, D_MODEL), BF16),
        compiler_params=_params("arbitrary"),
        name="prenorm",
    )(xp, xs, gains, mod4, mod4, mod4, mod4)


def _inproj_kernel(h_ref, w_ref, o_ref, wb_scr):
    @pl.when(pl.program_id(1) == 0)
    def _():
        wb_scr[...] = w_ref[0].astype(BF16)

    o_ref[...] = jnp.dot(h_ref[...], wb_scr[...], preferred_element_type=F32).astype(o_ref.dtype)


def _inproj_call(layer, h, w_in, tm, tn):
    rows = h.shape[0]
    return pl.pallas_call(
        _inproj_kernel,
        grid=(PROJ_WIDTH // tn, rows // tm),
        in_specs=[
            pl.BlockSpec((tm, D_MODEL), lambda j, i: (i, 0)),
            pl.BlockSpec((1, D_MODEL, tn), lambda j, i: (layer, 0, j)),
        ],
        out_specs=pl.BlockSpec((tm, tn), lambda j, i: (i, j)),
        out_shape=jax.ShapeDtypeStruct((rows, PROJ_WIDTH), BF16),
        scratch_shapes=[pltpu.VMEM((D_MODEL, tn), BF16)],
        compiler_params=_params("arbitrary", "arbitrary"),
        name="in_proj",
    )(h, w_in)


def _gateup_kernel(h_ref, wg_ref, wu_ref, wd_ref, o_ref, wdb_ref, wg_scr, wu_scr):
    @pl.when(pl.program_id(1) == 0)
    def _():
        wg_scr[...] = wg_ref[0].astype(BF16)
        wu_scr[...] = wu_ref[0].astype(BF16)

    wdb_ref[0] = wd_ref[0].astype(BF16)
    h = h_ref[...]
    a = jnp.dot(h, wg_scr[...], preferred_element_type=F32)
    b = jnp.dot(h, wu_scr[...], preferred_element_type=F32)
    o_ref[...] = (_silu(a) * b).astype(o_ref.dtype)


def _gateup_call(layer, h2, w_gate, w_up, w_down, tm, tn):
    rows = h2.shape[0]
    n_i = rows // tm
    wd_rows = tn // n_i
    assert tn % n_i == 0 and wd_rows % 16 == 0
    w_spec = pl.BlockSpec((1, D_MODEL, tn), lambda j, i: (layer, 0, j))
    return pl.pallas_call(
        _gateup_kernel,
        grid=(D_FF // tn, n_i),
        in_specs=[pl.BlockSpec((tm, D_MODEL), lambda j, i: (i, 0)), w_spec, w_spec,
                  pl.BlockSpec((1, wd_rows, D_MODEL), lambda j, i: (layer, j * n_i + i, 0))],
        out_specs=[pl.BlockSpec((tm, tn), lambda j, i: (i, j)),
                   pl.BlockSpec((1, wd_rows, D_MODEL), lambda j, i: (0, j * n_i + i, 0))],
        out_shape=[jax.ShapeDtypeStruct((rows, D_FF), BF16), jax.ShapeDtypeStruct((1, D_FF, D_MODEL), BF16)],
        scratch_shapes=[pltpu.VMEM((D_MODEL, tn), BF16), pltpu.VMEM((D_MODEL, tn), BF16)],
        compiler_params=_params("arbitrary", "arbitrary"),
        name="gate_up",
    )(h2, w_gate, w_up, w_down)


def _resid_kernel(*refs, tiles, sub, emit_h, a_split):
    refs = list(refs)
    ap_ref = refs.pop(0)
    as_ref = refs.pop(0) if a_split else ap_ref
    if emit_h:
        (w_ref, xp_ref, xs_ref, gain1_ref, gp_ref, gs_ref, gain2_ref, scp_ref, shp_ref, scs_ref, shs_ref,
         yp_ref, ys_ref, h_ref) = refs
    else:
        w_ref, xp_ref, xs_ref, gain1_ref, gp_ref, gs_ref, yp_ref, ys_ref = refs

    def run(is_sample):
        a_ref, x_ref, y_ref, g_ref = (as_ref, xs_ref, ys_ref, gs_ref) if is_sample else (ap_ref, xp_ref, yp_ref, gp_ref)
        for rows, shape3, seqs in _sub_blocks(tiles.tm, sub, is_sample):
            o = jnp.dot(a_ref[rows, :], w_ref[0], preferred_element_type=F32).reshape(shape3)
            gated_gain = gain1_ref[...] * g_ref[0, seqs]
            y = x_ref[rows, :].reshape(shape3) + o * lax.rsqrt(jnp.mean(o * o, axis=-1, keepdims=True) + EPS) * gated_gain
            y_ref[rows, :] = y.reshape(sub, D_MODEL)
            if emit_h:
                sc_ref, sh_ref = (scs_ref, shs_ref) if is_sample else (scp_ref, shp_ref)
                scaled_gain = gain2_ref[...] * (1.0 + sc_ref[0, seqs])
                h = y * lax.rsqrt(jnp.mean(y * y, axis=-1, keepdims=True) + EPS) * scaled_gain + sh_ref[0, seqs]
                h_ref[rows, :] = h.reshape(sub, D_MODEL).astype(BF16)

    _for_each_group(tiles.n_prompt, run)


def _resid_call(name, tiles, sub, a, w_b, xp, xs, gains, mod4, layer, gain1, gate_chunk, h_params):
    emit_h = h_params is not None
    a_split = isinstance(a, tuple)
    k_dim = w_b.shape[1]
    if a_split:
        in_specs, args = [tiles.prompt_rows(k_dim), tiles.sample_rows(k_dim)], list(a)
    else:
        in_specs, args = [tiles.all_rows(k_dim)], [a]
    in_specs += [
        _const_weight_spec(0, k_dim, D_MODEL),
        tiles.prompt_rows(D_MODEL), tiles.sample_rows(D_MODEL),
        _gain_spec(layer, gain1), tiles.prompt_mod(layer, gate_chunk), tiles.sample_mod(layer, gate_chunk),
    ]
    args += [w_b, xp, xs, gains, mod4, mod4]
    out_specs = [tiles.prompt_rows(D_MODEL), tiles.sample_rows(D_MODEL)]
    out_shape = [jax.ShapeDtypeStruct(xp.shape, F32), jax.ShapeDtypeStruct(xs.shape, F32)]
    if emit_h:
        hl, hg, hsc, hsh = h_params
        in_specs += [_gain_spec(hl, hg), tiles.prompt_mod(hl, hsc), tiles.prompt_mod(hl, hsh),
                     tiles.sample_mod(hl, hsc), tiles.sample_mod(hl, hsh)]
        args += [gains, mod4, mod4, mod4, mod4]
        out_specs.append(tiles.all_rows(D_MODEL))
        out_shape.append(jax.ShapeDtypeStruct((tiles.n_tiles * tiles.tm, D_MODEL), BF16))
    return pl.pallas_call(
        functools.partial(_resid_kernel, tiles=tiles, sub=sub, emit_h=emit_h, a_split=a_split),
        grid=(tiles.n_tiles,),
        in_specs=in_specs,
        out_specs=out_specs,
        out_shape=out_shape,
        compiler_params=_params("arbitrary"),
        name=name,
    )(*args)


def _mixer_tables(t_rows, t_valid, first_chunk_variant):
    qi = np.arange(t_rows)[:, None]
    kj = np.arange(2 * CHUNK)[None, :]
    dist = qi + CHUNK - kj
    allowed = (dist >= 0) & (dist < WINDOW)
    variants = [allowed & (kj >= CHUNK), allowed] if first_chunk_variant else [allowed]
    bias = np.stack([np.stack([np.where(ok, -slope * dist.astype(np.float64), NEG_INF) for slope in ALIBI_SLOPES])
                     for ok in variants])
    scale = RET_DK ** -0.5
    i = np.arange(t_rows)[:, None].astype(np.float64)
    j = np.arange(CHUNK)[None, :].astype(np.float64)
    jr = np.arange(CHUNK)[:, None].astype(np.float64)
    dtab = np.zeros((RET_HEADS, t_rows, CHUNK), np.float64)
    qtab = np.zeros((RET_HEADS, t_rows, CHUNK), np.float64)
    ktab = np.zeros((RET_HEADS, CHUNK, CHUNK), np.float64)
    sdec = []
    for h, lg in enumerate(LOG_GAMMA):
        causal = (i >= j) & (i < t_valid) & (j < t_valid)
        dtab[h] = np.where(causal, scale * np.exp(np.maximum(i - j, 0.0) * lg), 0.0)
        qtab[h] = np.where(i < t_valid, np.exp((i + 1.0) * lg), 0.0) * np.ones((1, CHUNK))
        ktab[h] = np.where(jr < t_valid, scale * np.exp(np.maximum(t_valid - 1.0 - jr, 0.0) * lg), 0.0) * np.ones((1, CHUNK))
        sdec.append(float(np.exp(t_valid * lg)))
    return tuple(jnp.asarray(t, F32) for t in (dtab, qtab, ktab, bias)), sdec


class _Chain:
    def __init__(self, proj, state, k_prev, v_prev, u_prev, mixed_ref, ubuf):
        self.proj, self.state, self.k_prev, self.v_prev, self.u_prev = proj, state, k_prev, v_prev, u_prev
        self.mixed_ref, self.ubuf = mixed_ref, ubuf


def _mixer_math(chains, tabs, bias_variant, sdec, conv_w, sinks):
    dtab_ref, qtab_ref, ktab_ref, bias_ref = tabs
    t_rows = chains[0].mixed_ref.shape[0]
    pad = CHUNK - t_rows
    nt = (((1,), (1,)), ((), ()))
    tn = (((0,), (0,)), ((), ()))
    att_base = RET_HEADS * RET_DV
    conv_base = att_base + ATT_Q_HEADS * ATT_HEAD_DIM
    att_scale = ATT_HEAD_DIM ** -0.5
    kv_lanes = [slice(g * ATT_HEAD_DIM, (g + 1) * ATT_HEAD_DIM) for g in range(ATT_KV_HEADS)]
    heads, att_heads = range(RET_HEADS), range(ATT_Q_HEADS)

    def pad_rows(a):
        if pad == 0:
            return a
        return jnp.concatenate([a, jnp.zeros((pad, a.shape[1]), a.dtype)], axis=0)

    for ch in chains:
        proj = ch.proj
        ch.k_cur = pad_rows(proj(OFF_AK, OFF_AK + 128))
        ch.v_cur = pad_rows(proj(OFF_AV, OFF_AV + 128))
        k_all = jnp.concatenate([ch.k_prev, ch.k_cur], axis=0)
        ch.v_all = jnp.concatenate([ch.v_prev, ch.v_cur], axis=0)
        rq = [proj(OFF_RQ + h * RET_DK, OFF_RQ + (h + 1) * RET_DK) for h in heads]
        rk = [pad_rows(proj(OFF_RK + h * RET_DK, OFF_RK + (h + 1) * RET_DK)) for h in heads]
        ch.rv = [pad_rows(proj(OFF_RV + h * RET_DV, OFF_RV + (h + 1) * RET_DV)) for h in heads]
        ch.ret_s = [lax.dot_general(rq[h], rk[h], nt, preferred_element_type=F32) for h in heads]
        ch.inter = [jnp.dot(rq[h], ch.state[h].astype(BF16), preferred_element_type=F32) for h in heads]
        ch.att_s = []
        for hh in att_heads:
            q = proj(OFF_AQ + hh * ATT_HEAD_DIM, OFF_AQ + (hh + 1) * ATT_HEAD_DIM) * att_scale
            ch.att_s.append(lax.dot_general(q, k_all[:, kv_lanes[hh // ATT_GROUP]], nt, preferred_element_type=F32))
        ch.new_state = []
        for h in heads:
            ks = (rk[h].astype(F32) * ktab_ref[h]).astype(BF16)
            kv = lax.dot_general(ks, ch.rv[h], tn, preferred_element_type=F32)
            ch.new_state.append(sdec[h] * ch.state[h] + kv)

    for ch in chains:
        ch.ret_p = [(ch.ret_s[h] * dtab_ref[h]).astype(BF16) for h in heads]
        ch.att_e, ch.att_den = [], []
        for hh in att_heads:
            s = ch.att_s[hh] + bias_ref[bias_variant, hh]
            sink = sinks[hh]
            m = jnp.maximum(jnp.max(s, axis=-1, keepdims=True), sink)
            e = jnp.exp(s - m)
            ch.att_den.append(jnp.sum(e, axis=-1, keepdims=True) + jnp.exp(sink - m))
            ch.att_e.append(e.astype(BF16))
        proj, ubuf = ch.proj, ch.ubuf
        gate_b = proj(OFF_CB, OFF_CB + CONV_CH).astype(F32)
        u = proj(OFF_CC, OFF_CC + CONV_CH).astype(F32) * proj(OFF_CH, OFF_CH + CONV_CH).astype(F32)
        ubuf[6:8, :] = ch.u_prev
        ubuf[8:8 + t_rows, :] = u
        y = conv_w[0:1, :] * ubuf[6:6 + t_rows, :] + conv_w[1:2, :] * ubuf[7:7 + t_rows, :] + conv_w[2:3, :] * u
        ch.mixed_ref[:, conv_base:conv_base + CONV_CH] = (gate_b * y).astype(ch.mixed_ref.dtype)

    for ch in chains:
        ch.intra = [jnp.dot(ch.ret_p[h], ch.rv[h], preferred_element_type=F32) for h in heads]
        ch.att_o = [jnp.dot(ch.att_e[hh], ch.v_all[:, kv_lanes[hh // ATT_GROUP]], preferred_element_type=F32)
                    for hh in att_heads]

    for ch in chains:
        for h in heads:
            gate = ch.proj(OFF_RG + h * RET_DV, OFF_RG + (h + 1) * RET_DV).astype(F32)
            ry = ch.intra[h] + ch.inter[h] * qtab_ref[h]
            ryn = ry * lax.rsqrt(jnp.mean(ry * ry, axis=-1, keepdims=True) + EPS)
            ch.mixed_ref[:, h * RET_DV:(h + 1) * RET_DV] = (_silu(gate) * ryn).astype(ch.mixed_ref.dtype)
        for hh in att_heads:
            c0 = att_base + hh * ATT_HEAD_DIM
            ch.mixed_ref[:, c0:c0 + ATT_HEAD_DIM] = (ch.att_o[hh] / ch.att_den[hh]).astype(ch.mixed_ref.dtype)
    return [(ch.new_state, ch.k_cur, ch.v_cur) for ch in chains]


def _mix_prompt_kernel(sink_ref, *refs, layer, sdec, batch):
    proj_refs, refs = refs[:batch], refs[batch:]
    (dtab_ref, qtab_ref, ktab_ref, bias_ref, convw_ref, wout_ref, mixed_ref, sret_ref, knew_ref, vnew_ref, cnew_ref,
     woutb_ref, s_scr, kprev_scr, vprev_scr, uprev_scr, ubuf) = refs
    c = pl.program_id(0)
    woutb_ref[0] = wout_ref[0].astype(BF16)

    @pl.when(c == 0)
    def _():
        s_scr[...] = jnp.zeros_like(s_scr)
        kprev_scr[...] = jnp.zeros_like(kprev_scr)
        vprev_scr[...] = jnp.zeros_like(vprev_scr)
        uprev_scr[...] = jnp.zeros_like(uprev_scr)

    sinks = [sink_ref[layer, hh] for hh in range(ATT_Q_HEADS)]
    bias_variant = jnp.where(c == 0, 0, 1)
    last = c == pl.num_programs(0) - 1
    chains = [_Chain(functools.partial(lambda ref, lo, hi: ref[:, lo:hi], proj_refs[b]),
                     [s_scr[b, h] for h in range(RET_HEADS)], kprev_scr[b], vprev_scr[b], uprev_scr[b],
                     mixed_ref.at[b], ubuf.at[b]) for b in range(batch)]
    results = [_mixer_math([ch], (dtab_ref, qtab_ref, ktab_ref, bias_ref), bias_variant, sdec, convw_ref[0], sinks)[0]
               for ch in chains]
    for b, (new_state, k_cur, v_cur) in enumerate(results):
        for h in range(RET_HEADS):
            s_scr[b, h] = new_state[h]
        kprev_scr[b] = k_cur
        vprev_scr[b] = v_cur
        uprev_scr[b] = ubuf[b, CHUNK + 6:CHUNK + 8, :]

        @pl.when(last)
        def _():
            for h in range(RET_HEADS):
                sret_ref[b, h] = new_state[h]
            knew_ref[b] = k_cur.astype(F32)
            vnew_ref[b] = v_cur.astype(F32)
            cnew_ref[b] = ubuf[b, CHUNK + 6:CHUNK + 8, :]


def _mix_prompt_call(layer, proj, batch, seq, conv_w, attn_sinks, tables, sdec, w_out):
    n_chunks = seq // CHUNK
    wrows = D_MODEL // n_chunks
    assert D_MODEL % n_chunks == 0 and wrows % 16 == 0
    const3 = lambda ci: (0, 0, 0)
    const4 = lambda ci: (0, 0, 0, 0)
    proj_specs = [pl.BlockSpec((CHUNK, PROJ_WIDTH), functools.partial(lambda b, ci: (b * n_chunks + ci, 0), b))
                  for b in range(batch)]
    return pl.pallas_call(
        functools.partial(_mix_prompt_kernel, layer=layer, sdec=sdec, batch=batch),
        grid=(n_chunks,),
        in_specs=[pl.BlockSpec(memory_space=pltpu.SMEM)] + proj_specs + [
            pl.BlockSpec((RET_HEADS, CHUNK, CHUNK), const3),
            pl.BlockSpec((RET_HEADS, CHUNK, CHUNK), const3),
            pl.BlockSpec((RET_HEADS, CHUNK, CHUNK), const3),
            pl.BlockSpec((2, ATT_Q_HEADS, CHUNK, 2 * CHUNK), const4),
            pl.BlockSpec((1, CONV_WIDTH, CONV_CH), lambda ci: (layer, 0, 0)),
            pl.BlockSpec((1, wrows, D_MODEL), lambda ci: (layer, ci, 0)),
        ],
        out_specs=[
            pl.BlockSpec((batch, CHUNK, D_MODEL), lambda ci: (0, ci, 0)),
            pl.BlockSpec((batch, RET_HEADS, RET_DK, RET_DV), const4),
            pl.BlockSpec((batch, WINDOW, 128), const3),
            pl.BlockSpec((batch, WINDOW, 128), const3),
            pl.BlockSpec((batch, CONV_WIDTH - 1, CONV_CH), const3),
            pl.BlockSpec((1, wrows, D_MODEL), lambda ci: (0, ci, 0)),
        ],
        out_shape=[
            jax.ShapeDtypeStruct((batch, seq, D_MODEL), BF16),
            jax.ShapeDtypeStruct((batch, RET_HEADS, RET_DK, RET_DV), F32),
            jax.ShapeDtypeStruct((batch, WINDOW, 128), F32),
            jax.ShapeDtypeStruct((batch, WINDOW, 128), F32),
            jax.ShapeDtypeStruct((batch, CONV_WIDTH - 1, CONV_CH), F32),
            jax.ShapeDtypeStruct((1, D_MODEL, D_MODEL), BF16),
        ],
        scratch_shapes=[
            pltpu.VMEM((batch, RET_HEADS, RET_DK, RET_DV), F32),
            pltpu.VMEM((batch, CHUNK, 128), BF16),
            pltpu.VMEM((batch, CHUNK, 128), BF16),
            pltpu.VMEM((batch, CONV_WIDTH - 1, CONV_CH), F32),
            pltpu.VMEM((batch, CHUNK + 8, CONV_CH), F32),
        ],
        compiler_params=_params("arbitrary"),
        name="mix_prompt",
    )(attn_sinks, *([proj] * batch), *tables, conv_w, w_out)


def _mix_sample_kernel(sink_ref, proj_ref, dtab_ref, qtab_ref, ktab_ref, bias_ref, convw_ref,
                       sin_ref, kbuf_ref, vbuf_ref, cbuf_ref,
                       mixed_ref, sret_ref, knew_ref, vnew_ref, cnew_ref,
                       ubuf, kvbuf, *, layer, sdec, t_valid, seqs_per_step):
    sinks = [sink_ref[layer, hh] for hh in range(ATT_Q_HEADS)]
    chains = []
    for s in range(seqs_per_step):
        rows = slice(s * SEQ_PAD, (s + 1) * SEQ_PAD)
        chains.append(_Chain(functools.partial(lambda r, lo, hi: proj_ref[r, lo:hi], rows),
                             [sin_ref[s, h] for h in range(RET_HEADS)], kbuf_ref[s].astype(BF16),
                             vbuf_ref[s].astype(BF16), cbuf_ref[s], mixed_ref.at[rows], ubuf.at[s]))
    results = _mixer_math(chains, (dtab_ref, qtab_ref, ktab_ref, bias_ref), 0, sdec, convw_ref[0], sinks)
    for s, (new_state, _, _) in enumerate(results):
        for h in range(RET_HEADS):
            sret_ref[s, h] = new_state[h]
        for j, (src_ref, off, dst_ref) in enumerate(((kbuf_ref, OFF_AK, knew_ref), (vbuf_ref, OFF_AV, vnew_ref))):
            kvbuf[s, j, 0:WINDOW, :] = src_ref[s]
            kvbuf[s, j, WINDOW:WINDOW + SEQ_PAD, :] = chains[s].proj(off, off + 128).astype(F32)
            dst_ref[s] = kvbuf[s, j, t_valid:t_valid + WINDOW, :]
        cnew_ref[s] = ubuf[s, 8 + t_valid - 2:8 + t_valid, :]


def _mix_sample_call(layer, proj, row0, conv_w, attn_sinks, tables, sdec, s_in, kbuf, vbuf, cbuf, t_valid):
    nb = s_in.shape[0]
    sps = 8
    assert nb % sps == 0 and row0 % (sps * SEQ_PAD) == 0
    blk0 = row0 // (sps * SEQ_PAD)
    const3 = lambda bi: (0, 0, 0)
    const4 = lambda bi: (0, 0, 0, 0)
    step3 = lambda bi: (bi, 0, 0)
    step4 = lambda bi: (bi, 0, 0, 0)
    return pl.pallas_call(
        functools.partial(_mix_sample_kernel, layer=layer, sdec=sdec, t_valid=t_valid, seqs_per_step=sps),
        grid=(nb // sps,),
        in_specs=[
            pl.BlockSpec(memory_space=pltpu.SMEM),
            pl.BlockSpec((sps * SEQ_PAD, PROJ_WIDTH), lambda bi: (blk0 + bi, 0)),
            pl.BlockSpec((RET_HEADS, SEQ_PAD, CHUNK), const3),
            pl.BlockSpec((RET_HEADS, SEQ_PAD, CHUNK), const3),
            pl.BlockSpec((RET_HEADS, CHUNK, CHUNK), const3),
            pl.BlockSpec((1, ATT_Q_HEADS, SEQ_PAD, 2 * CHUNK), const4),
            pl.BlockSpec((1, CONV_WIDTH, CONV_CH), lambda bi: (layer, 0, 0)),
            pl.BlockSpec((sps, RET_HEADS, RET_DK, RET_DV), step4),
            pl.BlockSpec((sps, WINDOW, 128), step3),
            pl.BlockSpec((sps, WINDOW, 128), step3),
            pl.BlockSpec((sps, CONV_WIDTH - 1, CONV_CH), step3),
        ],
        out_specs=[
            pl.BlockSpec((sps * SEQ_PAD, D_MODEL), lambda bi: (bi, 0)),
            pl.BlockSpec((sps, RET_HEADS, RET_DK, RET_DV), step4),
            pl.BlockSpec((sps, WINDOW, 128), step3),
            pl.BlockSpec((sps, WINDOW, 128), step3),
            pl.BlockSpec((sps, CONV_WIDTH - 1, CONV_CH), step3),
        ],
        out_shape=[
            jax.ShapeDtypeStruct((nb * SEQ_PAD, D_MODEL), BF16),
            jax.ShapeDtypeStruct((nb, RET_HEADS, RET_DK, RET_DV), F32),
            jax.ShapeDtypeStruct((nb, WINDOW, 128), F32),
            jax.ShapeDtypeStruct((nb, WINDOW, 128), F32),
            jax.ShapeDtypeStruct((nb, CONV_WIDTH - 1, CONV_CH), F32),
        ],
        scratch_shapes=[
            pltpu.VMEM((sps, SEQ_PAD + 8, CONV_CH), F32),
            pltpu.VMEM((sps, 2, WINDOW + SEQ_PAD, 128), F32),
        ],
        compiler_params=_params("arbitrary"),
        name="mix_sample",
    )(attn_sinks, proj, *tables, conv_w, s_in, kbuf, vbuf, cbuf)


def kernel(x_prompt, x_sample, state_ret, cache_win_k, cache_win_v, state_conv, c_prompt, c_sample,
           w_in, w_out, conv_w, attn_sinks, norm_g, w_ada, b_ada, w_ff_gate, w_ff_up, w_ff_down):
    batch, seq, _ = x_prompt.shape
    nb, t_valid, _ = x_sample.shape
    w_buf = cache_win_k.shape[2]
    assert w_buf == WINDOW and seq % CHUNK == 0 and t_valid <= SEQ_PAD and nb + batch <= MOD_ROWS
    rows_p, rows_s = batch * seq, nb * SEQ_PAD
    rows = rows_p + rows_s

    c_all = jnp.concatenate([c_sample, c_prompt, jnp.zeros((MOD_ROWS - nb - batch, D_MODEL), F32)], axis=0)
    mod4 = _ada_call(c_all, w_ada, b_ada).reshape(DEPTH, MOD_ROWS, 1, N_MOD * D_MODEL)
    gains = norm_g.reshape(DEPTH * 4, 1, D_MODEL)

    tiles_out = tiles_down = _RowTiles(256, rows_p, rows_s, seq, nb)
    tm_stream = rows // 8
    assert rows % 8 == 0 and tm_stream % SEQ_PAD == 0
    tabs_p, sdec_p = _mixer_tables(CHUNK, CHUNK, True)
    tabs_s, sdec_s = _mixer_tables(SEQ_PAD, t_valid, False)

    xp = x_prompt.reshape(rows_p, D_MODEL)
    xs = jnp.pad(x_sample, ((0, 0), (0, SEQ_PAD - t_valid), (0, 0))).reshape(rows_s, D_MODEL)
    kbuf = cache_win_k.reshape(DEPTH, nb, w_buf, 128)
    vbuf = cache_win_v.reshape(DEPTH, nb, w_buf, 128)

    h = _prenorm_call(_RowTiles(512, rows_p, rows_s, seq, nb), 0, xp, xs, gains, mod4)
    outs_p, outs_s = [], []
    for l in range(DEPTH):
        proj = _inproj_call(l, h, w_in, tm_stream, 1280)
        mixed_p, *st_p, w_out_b = _mix_prompt_call(l, proj, batch, seq, conv_w, attn_sinks, tabs_p, sdec_p, w_out)
        mixed_s, *st_s = _mix_sample_call(l, proj, rows_p, conv_w, attn_sinks, tabs_s, sdec_s,
                                          state_ret[l], kbuf[l], vbuf[l], state_conv[l], t_valid)
        mixed = (mixed_p.reshape(rows_p, D_MODEL), mixed_s)
        xp, xs, h2 = _resid_call("out_proj", tiles_out, 128, mixed, w_out_b, xp, xs, gains, mod4, l,
                                 1, MOD_G1, (l, 2, MOD_SC2, MOD_SH2))
        hid, w_down_b = _gateup_call(l, h2, w_ff_gate, w_ff_up, w_ff_down, tm_stream, 512)
        next_h = (l + 1, 0, MOD_SC1, MOD_SH1) if l + 1 < DEPTH else None
        res = _resid_call("ffn_down", tiles_down, 128, hid, w_down_b, xp, xs, gains, mod4, l,
                          3, MOD_G2, next_h)
        xp, xs = res[0], res[1]
        if next_h is not None:
            h = res[2]
        outs_p.append(st_p)
        outs_s.append(st_s)

    def stack(outs, idx, shape):
        return jnp.stack([o[idx] for o in outs]).reshape(shape)

    kv_shape_p = (DEPTH, batch, w_buf, ATT_KV_HEADS, ATT_HEAD_DIM)
    kv_shape_s = (DEPTH, nb, w_buf, ATT_KV_HEADS, ATT_HEAD_DIM)
    return (
        xp.reshape(batch, seq, D_MODEL),
        xs.reshape(nb, SEQ_PAD, D_MODEL)[:, :t_valid],
        stack(outs_p, 0, (DEPTH, batch, RET_HEADS, RET_DK, RET_DV)),
        stack(outs_s, 0, (DEPTH, nb, RET_HEADS, RET_DK, RET_DV)),
        stack(outs_p, 1, kv_shape_p),
        stack(outs_s, 1, kv_shape_s),
        stack(outs_p, 2, kv_shape_p),
        stack(outs_s, 2, kv_shape_s),
        stack(outs_p, 3, (DEPTH, batch, CONV_WIDTH - 1, CONV_CH)),
        stack(outs_s, 3, (DEPTH, nb, CONV_WIDTH - 1, CONV_CH)),
    )
```

```python
import functools
import math

import numpy as np
import jax
import jax.numpy as jnp
from jax import lax
from jax.experimental import pallas as pl
from jax.experimental.pallas import tpu as pltpu

F32 = jnp.float32
BF16 = jnp.bfloat16

D_MODEL = 2048
DEPTH = 4
RET_DK = 128
RET_DV = 128
RET_HEADS = 8
CHUNK = 128
ATT_HEAD_DIM = 64
ATT_Q_HEADS = 8
ATT_KV_HEADS = 2
ATT_GROUP = 4
WINDOW = 128
CONV_WIDTH = 3
CONV_CH = 512
D_FF = 5632
N_MOD = 6
EPS = 1e-6
NEG_INF = -1e30
PROJ_WIDTH = 6400
OFF_RQ, OFF_RK, OFF_RV, OFF_RG = 0, 1024, 2048, 3072
OFF_AQ, OFF_AK, OFF_AV = 4096, 4608, 4736
OFF_CB, OFF_CC, OFF_CH = 4864, 5376, 5888
MOD_SH1, MOD_SC1, MOD_G1, MOD_SH2, MOD_SC2, MOD_G2 = range(6)

SEQ_PAD = 8
MIX_ROWS = 16
MOD_ROWS = 48
VMEM_LIMIT_V7X = 56 * 1024 * 1024

LOG_GAMMA = [math.log1p(-(2.0 ** (-5.0 - h))) for h in range(RET_HEADS)]
ALIBI_SLOPES = [2.0 ** (-8.0 * (h + 1) / ATT_Q_HEADS) for h in range(ATT_Q_HEADS)]


def _silu(a):
    return a / (1.0 + jnp.exp(-a))


def _rms(x, gain):
    return x * lax.rsqrt(jnp.mean(x * x, axis=-1, keepdims=True) + EPS) * gain


def _norm_mod(x, gain, scale, shift):
    return _rms(x, gain) * (1.0 + scale) + shift


def _params(*sem):
    return pltpu.CompilerParams(dimension_semantics=sem, vmem_limit_bytes=VMEM_LIMIT_V7X)


def _ada_kernel(c_ref, w_ref, b_ref, o_ref):
    a = _silu(c_ref[...]).astype(BF16)
    o_ref[0] = jnp.dot(a, w_ref[0].astype(BF16), preferred_element_type=F32) + b_ref[0]


def _ada_call(c_all, w_ada, b_ada):
    tn = 1024
    n = N_MOD * D_MODEL
    return pl.pallas_call(
        _ada_kernel,
        grid=(DEPTH, n // tn),
        in_specs=[
            pl.BlockSpec((MOD_ROWS, D_MODEL), lambda l, j: (0, 0)),
            pl.BlockSpec((1, D_MODEL, tn), lambda l, j: (l, 0, j)),
            pl.BlockSpec((1, 1, tn), lambda l, j: (l, 0, j)),
        ],
        out_specs=pl.BlockSpec((1, MOD_ROWS, tn), lambda l, j: (l, 0, j)),
        out_shape=jax.ShapeDtypeStruct((DEPTH, MOD_ROWS, n), F32),
        compiler_params=_params("arbitrary", "arbitrary"),
        name="ada_mod",
    )(c_all, w_ada, b_ada.reshape(DEPTH, 1, n))


class _RowTiles:
    def __init__(self, tm, rows_prompt, rows_sample, seq_len, n_seq):
        assert rows_prompt % tm == 0 and rows_sample % tm == 0 and seq_len % tm == 0 and tm % SEQ_PAD == 0
        self.tm = tm
        self.n_prompt = rows_prompt // tm
        self.n_sample = rows_sample // tm
        self.n_tiles = self.n_prompt + self.n_sample
        self.tiles_per_seq = seq_len // tm
        self.seqs_per_tile = tm // SEQ_PAD
        self.prompt_mod_row0 = n_seq

    def all_rows(self, width):
        return pl.BlockSpec((self.tm, width), lambda i: (i, 0))

    def prompt_rows(self, width):
        last = self.n_prompt - 1
        return pl.BlockSpec((self.tm, width), lambda i: (jnp.minimum(i, last), 0))

    def sample_rows(self, width):
        first = self.n_prompt
        return pl.BlockSpec((self.tm, width), lambda i: (jnp.maximum(i - first, 0), 0))

    def prompt_mod(self, layer, chunk):
        last, per, row0 = self.n_prompt - 1, self.tiles_per_seq, self.prompt_mod_row0
        return pl.BlockSpec((1, 1, 1, D_MODEL), lambda i: (layer, row0 + jnp.minimum(i, last) // per, 0, chunk))

    def sample_mod(self, layer, chunk):
        first = self.n_prompt
        return pl.BlockSpec((1, self.seqs_per_tile, 1, D_MODEL),
                            lambda i: (layer, jnp.maximum(i - first, 0), 0, chunk))


def _gain_spec(layer, which):
    return pl.BlockSpec((1, 1, D_MODEL), lambda i: (layer * 4 + which, 0, 0))


def _const_weight_spec(layer, k, n):
    return pl.BlockSpec((1, k, n), lambda i: (layer, 0, 0), pipeline_mode=pl.Buffered(1))


def _for_each_group(n_prompt_tiles, fn):
    i = pl.program_id(0)

    @pl.when(i < n_prompt_tiles)
    def _():
        fn(False)

    @pl.when(i >= n_prompt_tiles)
    def _():
        fn(True)


def _sub_blocks(tm, sub, is_sample):
    for r in range(tm // sub):
        rows = slice(r * sub, (r + 1) * sub)
        if is_sample:
            g = sub // SEQ_PAD
            yield rows, (g, SEQ_PAD, D_MODEL), slice(r * g, (r + 1) * g)
        else:
            yield rows, (1, sub, D_MODEL), slice(0, 1)


def _prenorm_kernel(xp_ref, xs_ref, gain_ref, scp_ref, shp_ref, scs_ref, shs_ref, h_ref, *, tiles, sub):
    def run(is_sample):
        x_ref, sc_ref, sh_ref = (xs_ref, scs_ref, shs_ref) if is_sample else (xp_ref, scp_ref, shp_ref)
        for rows, shape3, seqs in _sub_blocks(tiles.tm, sub, is_sample):
            h = _norm_mod(x_ref[rows, :].reshape(shape3), gain_ref[...], sc_ref[0, seqs], sh_ref[0, seqs])
            h_ref[rows, :] = h.reshape(sub, D_MODEL).astype(BF16)

    _for_each_group(tiles.n_prompt, run)


def _prenorm_call(tiles, layer, xp, xs, gains, mod4):
    return pl.pallas_call(
        functools.partial(_prenorm_kernel, tiles=tiles, sub=128),
        grid=(tiles.n_tiles,),
        in_specs=[
            tiles.prompt_rows(D_MODEL), tiles.sample_rows(D_MODEL), _gain_spec(layer, 0),
            tiles.prompt_mod(layer, MOD_SC1), tiles.prompt_mod(layer, MOD_SH1),
            tiles.sample_mod(layer, MOD_SC1), tiles.sample_mod(layer, MOD_SH1),
        ],
        out_specs=tiles.all_rows(D_MODEL),
        out_shape=jax.ShapeDtypeStruct((tiles.n_tiles * tiles.tm, D_MODEL), BF16),
        compiler_params=_params("arbitrary"),
        name="prenorm",
    )(xp, xs, gains, mod4, mod4, mod4, mod4)


def _inproj_kernel(h_ref, w_ref, o_ref, wb_scr):
    @pl.when(pl.program_id(1) == 0)
    def _():
        wb_scr[...] = w_ref[0].astype(BF16)

    o_ref[...] = jnp.dot(h_ref[...], wb_scr[...], preferred_element_type=F32).astype(o_ref.dtype)


def _inproj_call(layer, h, w_in, tm, tn):
    rows = h.shape[0]
    return pl.pallas_call(
        _inproj_kernel,
        grid=(PROJ_WIDTH // tn, rows // tm),
        in_specs=[
            pl.BlockSpec((tm, D_MODEL), lambda j, i: (i, 0)),
            pl.BlockSpec((1, D_MODEL, tn), lambda j, i: (layer, 0, j)),
        ],
        out_specs=pl.BlockSpec((tm, tn), lambda j, i: (i, j)),
        out_shape=jax.ShapeDtypeStruct((rows, PROJ_WIDTH), BF16),
        scratch_shapes=[pltpu.VMEM((D_MODEL, tn), BF16)],
        compiler_params=_params("arbitrary", "arbitrary"),
        name="in_proj",
    )(h, w_in)


def _gateup_kernel(h_ref, wg_ref, wu_ref, wd_ref, o_ref, wdb_ref, wg_scr, wu_scr):
    @pl.when(pl.program_id(1) == 0)
    def _():
        wg_scr[...] = wg_ref[0].astype(BF16)
        wu_scr[...] = wu_ref[0].astype(BF16)

    wdb_ref[0] = wd_ref[0].astype(BF16)
    h = h_ref[...]
    a = jnp.dot(h, wg_scr[...], preferred_element_type=F32)
    b = jnp.dot(h, wu_scr[...], preferred_element_type=F32)
    o_ref[...] = (_silu(a) * b).astype(o_ref.dtype)


def _gateup_call(layer, h2, w_gate, w_up, w_down, tm, tn):
    rows = h2.shape[0]
    n_i = rows // tm
    wd_rows = tn // n_i
    assert tn % n_i == 0 and wd_rows % 16 == 0
    w_spec = pl.BlockSpec((1, D_MODEL, tn), lambda j, i: (layer, 0, j))
    return pl.pallas_call(
        _gateup_kernel,
        grid=(D_FF // tn, n_i),
        in_specs=[pl.BlockSpec((tm, D_MODEL), lambda j, i: (i, 0)), w_spec, w_spec,
                  pl.BlockSpec((1, wd_rows, D_MODEL), lambda j, i: (layer, j * n_i + i, 0))],
        out_specs=[pl.BlockSpec((tm, tn), lambda j, i: (i, j)),
                   pl.BlockSpec((1, wd_rows, D_MODEL), lambda j, i: (0, j * n_i + i, 0))],
        out_shape=[jax.ShapeDtypeStruct((rows, D_FF), BF16), jax.ShapeDtypeStruct((1, D_FF, D_MODEL), BF16)],
        scratch_shapes=[pltpu.VMEM((D_MODEL, tn), BF16), pltpu.VMEM((D_MODEL, tn), BF16)],
        compiler_params=_params("arbitrary", "arbitrary"),
        name="gate_up",
    )(h2, w_gate, w_up, w_down)


def _resid_kernel(*refs, tiles, sub, emit_h, a_split):
    refs = list(refs)
    ap_ref = refs.pop(0)
    as_ref = refs.pop(0) if a_split else ap_ref
    if emit_h:
        (w_ref, xp_ref, xs_ref, gain1_ref, gp_ref, gs_ref, gain2_ref, scp_ref, shp_ref, scs_ref, shs_ref,
         yp_ref, ys_ref, h_ref) = refs
    else:
        w_ref, xp_ref, xs_ref, gain1_ref, gp_ref, gs_ref, yp_ref, ys_ref = refs

    def run(is_sample):
        a_ref, x_ref, y_ref, g_ref = (as_ref, xs_ref, ys_ref, gs_ref) if is_sample else (ap_ref, xp_ref, yp_ref, gp_ref)
        for rows, shape3, seqs in _sub_blocks(tiles.tm, sub, is_sample):
            o = jnp.dot(a_ref[rows, :], w_ref[0], preferred_element_type=F32).reshape(shape3)
            gated_gain = gain1_ref[...] * g_ref[0, seqs]
            y = x_ref[rows, :].reshape(shape3) + o * lax.rsqrt(jnp.mean(o * o, axis=-1, keepdims=True) + EPS) * gated_gain
            y_ref[rows, :] = y.reshape(sub, D_MODEL)
            if emit_h:
                sc_ref, sh_ref = (scs_ref, shs_ref) if is_sample else (scp_ref, shp_ref)
                scaled_gain = gain2_ref[...] * (1.0 + sc_ref[0, seqs])
                h = y * lax.rsqrt(jnp.mean(y * y, axis=-1, keepdims=True) + EPS) * scaled_gain + sh_ref[0, seqs]
                h_ref[rows, :] = h.reshape(sub, D_MODEL).astype(BF16)

    _for_each_group(tiles.n_prompt, run)


def _resid_call(name, tiles, sub, a, w_b, xp, xs, gains, mod4, layer, gain1, gate_chunk, h_params):
    emit_h = h_params is not None
    a_split = isinstance(a, tuple)
    k_dim = w_b.shape[1]
    if a_split:
        in_specs, args = [tiles.prompt_rows(k_dim), tiles.sample_rows(k_dim)], list(a)
    else:
        in_specs, args = [tiles.all_rows(k_dim)], [a]
    in_specs += [
        _const_weight_spec(0, k_dim, D_MODEL),
        tiles.prompt_rows(D_MODEL), tiles.sample_rows(D_MODEL),
        _gain_spec(layer, gain1), tiles.prompt_mod(layer, gate_chunk), tiles.sample_mod(layer, gate_chunk),
    ]
    args += [w_b, xp, xs, gains, mod4, mod4]
    out_specs = [tiles.prompt_rows(D_MODEL), tiles.sample_rows(D_MODEL)]
    out_shape = [jax.ShapeDtypeStruct(xp.shape, F32), jax.ShapeDtypeStruct(xs.shape, F32)]
    if emit_h:
        hl, hg, hsc, hsh = h_params
        in_specs += [_gain_spec(hl, hg), tiles.prompt_mod(hl, hsc), tiles.prompt_mod(hl, hsh),
                     tiles.sample_mod(hl, hsc), tiles.sample_mod(hl, hsh)]
        args += [gains, mod4, mod4, mod4, mod4]
        out_specs.append(tiles.all_rows(D_MODEL))
        out_shape.append(jax.ShapeDtypeStruct((tiles.n_tiles * tiles.tm, D_MODEL), BF16))
    return pl.pallas_call(
        functools.partial(_resid_kernel, tiles=tiles, sub=sub, emit_h=emit_h, a_split=a_split),
        grid=(tiles.n_tiles,),
        in_specs=in_specs,
        out_specs=out_specs,
        out_shape=out_shape,
        compiler_params=_params("arbitrary"),
        name=name,
    )(*args)


def _mixer_tables(t_rows, t_valid, first_chunk_variant):
    qi = np.arange(t_rows)[:, None]
    kj = np.arange(2 * CHUNK)[None, :]
    dist = qi + CHUNK - kj
    allowed = (dist >= 0) & (dist < WINDOW)
    variants = [allowed & (kj >= CHUNK), allowed] if first_chunk_variant else [allowed]
    bias = np.stack([np.stack([np.where(ok, -slope * dist.astype(np.float64), NEG_INF) for slope in ALIBI_SLOPES])
                     for ok in variants])
    scale = RET_DK ** -0.5
    i = np.arange(t_rows)[:, None].astype(np.float64)
    j = np.arange(CHUNK)[None, :].astype(np.float64)
    jr = np.arange(CHUNK)[:, None].astype(np.float64)
    dtab = np.zeros((RET_HEADS, t_rows, CHUNK), np.float64)
    qtab = np.zeros((RET_HEADS, t_rows, CHUNK), np.float64)
    ktab = np.zeros((RET_HEADS, CHUNK, CHUNK), np.float64)
    sdec = []
    for h, lg in enumerate(LOG_GAMMA):
        causal = (i >= j) & (i < t_valid) & (j < t_valid)
        dtab[h] = np.where(causal, scale * np.exp(np.maximum(i - j, 0.0) * lg), 0.0)
        qtab[h] = np.where(i < t_valid, np.exp((i + 1.0) * lg), 0.0) * np.ones((1, CHUNK))
        ktab[h] = np.where(jr < t_valid, scale * np.exp(np.maximum(t_valid - 1.0 - jr, 0.0) * lg), 0.0) * np.ones((1, CHUNK))
        sdec.append(float(np.exp(t_valid * lg)))
    return tuple(jnp.asarray(t, F32) for t in (dtab, qtab, ktab, bias)), sdec


class _Chain:
    def __init__(self, proj, state, k_prev, v_prev, u_prev, mixed_ref, ubuf):
        self.proj, self.state, self.k_prev, self.v_prev, self.u_prev = proj, state, k_prev, v_prev, u_prev
        self.mixed_ref, self.ubuf = mixed_ref, ubuf


def _mixer_math(chains, tabs, bias_variant, sdec, conv_w, sinks):
    dtab_ref, qtab_ref, ktab_ref, bias_ref = tabs
    t_rows = chains[0].mixed_ref.shape[0]
    pad = CHUNK - t_rows
    nt = (((1,), (1,)), ((), ()))
    tn = (((0,), (0,)), ((), ()))
    att_base = RET_HEADS * RET_DV
    conv_base = att_base + ATT_Q_HEADS * ATT_HEAD_DIM
    att_scale = ATT_HEAD_DIM ** -0.5
    kv_lanes = [slice(g * ATT_HEAD_DIM, (g + 1) * ATT_HEAD_DIM) for g in range(ATT_KV_HEADS)]
    heads, att_heads = range(RET_HEADS), range(ATT_Q_HEADS)

    def pad_rows(a):
        if pad == 0:
            return a
        return jnp.concatenate([a, jnp.zeros((pad, a.shape[1]), a.dtype)], axis=0)

    for ch in chains:
        proj = ch.proj
        ch.k_cur = pad_rows(proj(OFF_AK, OFF_AK + 128))
        ch.v_cur = pad_rows(proj(OFF_AV, OFF_AV + 128))
        k_all = jnp.concatenate([ch.k_prev, ch.k_cur], axis=0)
        ch.v_all = jnp.concatenate([ch.v_prev, ch.v_cur], axis=0)
        rq = [proj(OFF_RQ + h * RET_DK, OFF_RQ + (h + 1) * RET_DK) for h in heads]
        rk = [pad_rows(proj(OFF_RK + h * RET_DK, OFF_RK + (h + 1) * RET_DK)) for h in heads]
        ch.rv = [pad_rows(proj(OFF_RV + h * RET_DV, OFF_RV + (h + 1) * RET_DV)) for h in heads]
        ch.ret_s = [lax.dot_general(rq[h], rk[h], nt, preferred_element_type=F32) for h in heads]
        ch.inter = [jnp.dot(rq[h], ch.state[h].astype(BF16), preferred_element_type=F32) for h in heads]
        ch.att_s = []
        for hh in att_heads:
            q = proj(OFF_AQ + hh * ATT_HEAD_DIM, OFF_AQ + (hh + 1) * ATT_HEAD_DIM) * att_scale
            ch.att_s.append(lax.dot_general(q, k_all[:, kv_lanes[hh // ATT_GROUP]], nt, preferred_element_type=F32))
        ch.new_state = []
        for h in heads:
            ks = (rk[h].astype(F32) * ktab_ref[h]).astype(BF16)
            kv = lax.dot_general(ks, ch.rv[h], tn, preferred_element_type=F32)
            ch.new_state.append(sdec[h] * ch.state[h] + kv)

    for ch in chains:
        ch.ret_p = [(ch.ret_s[h] * dtab_ref[h]).astype(BF16) for h in heads]
        ch.att_e, ch.att_den = [], []
        for hh in att_heads:
            s = ch.att_s[hh] + bias_ref[bias_variant, hh]
            sink = sinks[hh]
            m = jnp.maximum(jnp.max(s, axis=-1, keepdims=True), sink)
            e = jnp.exp(s - m)
            ch.att_den.append(jnp.sum(e, axis=-1, keepdims=True) + jnp.exp(sink - m))
            ch.att_e.append(e.astype(BF16))
        proj, ubuf = ch.proj, ch.ubuf
        gate_b = proj(OFF_CB, OFF_CB + CONV_CH).astype(F32)
        u = proj(OFF_CC, OFF_CC + CONV_CH).astype(F32) * proj(OFF_CH, OFF_CH + CONV_CH).astype(F32)
        ubuf[6:8, :] = ch.u_prev
        ubuf[8:8 + t_rows, :] = u
        y = conv_w[0:1, :] * ubuf[6:6 + t_rows, :] + conv_w[1:2, :] * ubuf[7:7 + t_rows, :] + conv_w[2:3, :] * u
        ch.mixed_ref[:, conv_base:conv_base + CONV_CH] = (gate_b * y).astype(ch.mixed_ref.dtype)

    for ch in chains:
        ch.intra = [jnp.dot(ch.ret_p[h], ch.rv[h], preferred_element_type=F32) for h in heads]
        ch.att_o = [jnp.dot(ch.att_e[hh], ch.v_all[:, kv_lanes[hh // ATT_GROUP]], preferred_element_type=F32)
                    for hh in att_heads]

    for ch in chains:
        for h in heads:
            gate = ch.proj(OFF_RG + h * RET_DV, OFF_RG + (h + 1) * RET_DV).astype(F32)
            ry = ch.intra[h] + ch.inter[h] * qtab_ref[h]
            ryn = ry * lax.rsqrt(jnp.mean(ry * ry, axis=-1, keepdims=True) + EPS)
            ch.mixed_ref[:, h * RET_DV:(h + 1) * RET_DV] = (_silu(gate) * ryn).astype(ch.mixed_ref.dtype)
        for hh in att_heads:
            c0 = att_base + hh * ATT_HEAD_DIM
            ch.mixed_ref[:, c0:c0 + ATT_HEAD_DIM] = (ch.att_o[hh] / ch.att_den[hh]).astype(ch.mixed_ref.dtype)
    return [(ch.new_state, ch.k_cur, ch.v_cur) for ch in chains]


def _mix_prompt_kernel(sink_ref, *refs, layer, sdec, batch):
    proj_refs, refs = refs[:batch], refs[batch:]
    (dtab_ref, qtab_ref, ktab_ref, bias_ref, convw_ref, wout_ref, mixed_ref, sret_ref, knew_ref, vnew_ref, cnew_ref,
     woutb_ref, s_scr, kprev_scr, vprev_scr, uprev_scr, ubuf) = refs
    c = pl.program_id(0)
    woutb_ref[0] = wout_ref[0].astype(BF16)

    @pl.when(c == 0)
    def _():
        s_scr[...] = jnp.zeros_like(s_scr)
        kprev_scr[...] = jnp.zeros_like(kprev_scr)
        vprev_scr[...] = jnp.zeros_like(vprev_scr)
        uprev_scr[...] = jnp.zeros_like(uprev_scr)

    sinks = [sink_ref[layer, hh] for hh in range(ATT_Q_HEADS)]
    bias_variant = jnp.where(c == 0, 0, 1)
    last = c == pl.num_programs(0) - 1
    chains = [_Chain(functools.partial(lambda ref, lo, hi: ref[:, lo:hi], proj_refs[b]),
                     [s_scr[b, h] for h in range(RET_HEADS)], kprev_scr[b], vprev_scr[b], uprev_scr[b],
                     mixed_ref.at[b], ubuf.at[b]) for b in range(batch)]
    results = [_mixer_math([ch], (dtab_ref, qtab_ref, ktab_ref, bias_ref), bias_variant, sdec, convw_ref[0], sinks)[0]
               for ch in chains]
    for b, (new_state, k_cur, v_cur) in enumerate(results):
        for h in range(RET_HEADS):
            s_scr[b, h] = new_state[h]
        kprev_scr[b] = k_cur
        vprev_scr[b] = v_cur
        uprev_scr[b] = ubuf[b, CHUNK + 6:CHUNK + 8, :]

        @pl.when(last)
        def _():
            for h in range(RET_HEADS):
                sret_ref[b, h] = new_state[h]
            knew_ref[b] = k_cur.astype(F32)
            vnew_ref[b] = v_cur.astype(F32)
            cnew_ref[b] = ubuf[b, CHUNK + 6:CHUNK + 8, :]


def _mix_prompt_call(layer, proj, batch, seq, conv_w, attn_sinks, tables, sdec, w_out):
    n_chunks = seq // CHUNK
    wrows = D_MODEL // n_chunks
    assert D_MODEL % n_chunks == 0 and wrows % 16 == 0
    const3 = lambda ci: (0, 0, 0)
    const4 = lambda ci: (0, 0, 0, 0)
    proj_specs = [pl.BlockSpec((CHUNK, PROJ_WIDTH), functools.partial(lambda b, ci: (b * n_chunks + ci, 0), b))
                  for b in range(batch)]
    return pl.pallas_call(
        functools.partial(_mix_prompt_kernel, layer=layer, sdec=sdec, batch=batch),
        grid=(n_chunks,),
        in_specs=[pl.BlockSpec(memory_space=pltpu.SMEM)] + proj_specs + [
            pl.BlockSpec((RET_HEADS, CHUNK, CHUNK), const3),
            pl.BlockSpec((RET_HEADS, CHUNK, CHUNK), const3),
            pl.BlockSpec((RET_HEADS, CHUNK, CHUNK), const3),
            pl.BlockSpec((2, ATT_Q_HEADS, CHUNK, 2 * CHUNK), const4),
            pl.BlockSpec((1, CONV_WIDTH, CONV_CH), lambda ci: (layer, 0, 0)),
            pl.BlockSpec((1, wrows, D_MODEL), lambda ci: (layer, ci, 0)),
        ],
        out_specs=[
            pl.BlockSpec((batch, CHUNK, D_MODEL), lambda ci: (0, ci, 0)),
            pl.BlockSpec((batch, RET_HEADS, RET_DK, RET_DV), const4),
            pl.BlockSpec((batch, WINDOW, 128), const3),
            pl.BlockSpec((batch, WINDOW, 128), const3),
            pl.BlockSpec((batch, CONV_WIDTH - 1, CONV_CH), const3),
            pl.BlockSpec((1, wrows, D_MODEL), lambda ci: (0, ci, 0)),
        ],
        out_shape=[
            jax.ShapeDtypeStruct((batch, seq, D_MODEL), BF16),
            jax.ShapeDtypeStruct((batch, RET_HEADS, RET_DK, RET_DV), F32),
            jax.ShapeDtypeStruct((batch, WINDOW, 128), F32),
            jax.ShapeDtypeStruct((batch, WINDOW, 128), F32),
            jax.ShapeDtypeStruct((batch, CONV_WIDTH - 1, CONV_CH), F32),
            jax.ShapeDtypeStruct((1, D_MODEL, D_MODEL), BF16),
        ],
        scratch_shapes=[
            pltpu.VMEM((batch, RET_HEADS, RET_DK, RET_DV), F32),
            pltpu.VMEM((batch, CHUNK, 128), BF16),
            pltpu.VMEM((batch, CHUNK, 128), BF16),
            pltpu.VMEM((batch, CONV_WIDTH - 1, CONV_CH), F32),
            pltpu.VMEM((batch, CHUNK + 8, CONV_CH), F32),
        ],
        compiler_params=_params("arbitrary"),
        name="mix_prompt",
    )(attn_sinks, *([proj] * batch), *tables, conv_w, w_out)


def _mix_sample_kernel(sink_ref, proj_ref, dtab_ref, qtab_ref, ktab_ref, bias_ref, convw_ref,
                       sin_ref, kbuf_ref, vbuf_ref, cbuf_ref,
                       mixed_ref, sret_ref, knew_ref, vnew_ref, cnew_ref,
                       ubuf, kvbuf, mix_scr, *, layer, sdec, t_valid, seqs_per_step):
    assert 2 * SEQ_PAD == MIX_ROWS and seqs_per_step % 2 == 0

    def seq_proj(s, lo, hi):
        pair = proj_ref[(s // 2) * MIX_ROWS:(s // 2 + 1) * MIX_ROWS, lo:hi].astype(F32)
        own = pair[(s % 2) * SEQ_PAD:(s % 2 + 1) * SEQ_PAD]
        return jnp.concatenate([own, jnp.zeros((MIX_ROWS - SEQ_PAD, hi - lo), F32)], axis=0).astype(BF16)

    sinks = [sink_ref[layer, hh] for hh in range(ATT_Q_HEADS)]
    chains = [_Chain(functools.partial(seq_proj, s), [sin_ref[s, h] for h in range(RET_HEADS)],
                     kbuf_ref[s].astype(BF16), vbuf_ref[s].astype(BF16), cbuf_ref[s], mix_scr.at[s], ubuf.at[s])
              for s in range(seqs_per_step)]
    results = _mixer_math(chains, (dtab_ref, qtab_ref, ktab_ref, bias_ref), 0, sdec, convw_ref[0], sinks)
    for s, (new_state, _, _) in enumerate(results):
        for h in range(RET_HEADS):
            sret_ref[s, h] = new_state[h]
        for j, (src_ref, off, dst_ref) in enumerate(((kbuf_ref, OFF_AK, knew_ref), (vbuf_ref, OFF_AV, vnew_ref))):
            kvbuf[s, j, 0:WINDOW, :] = src_ref[s]
            kvbuf[s, j, WINDOW:WINDOW + MIX_ROWS, :] = seq_proj(s, off, off + 128).astype(F32)
            dst_ref[s] = kvbuf[s, j, t_valid:t_valid + WINDOW, :]
        cnew_ref[s] = ubuf[s, 8 + t_valid - 2:8 + t_valid, :]
    for p in range(seqs_per_step // 2):
        pair = jnp.concatenate([mix_scr[2 * p, 0:SEQ_PAD, :], mix_scr[2 * p + 1, 0:SEQ_PAD, :]], axis=0)
        mixed_ref[p * MIX_ROWS:(p + 1) * MIX_ROWS, :] = pair.astype(mixed_ref.dtype)


def _mix_sample_call(layer, proj, row0, conv_w, attn_sinks, tables, sdec, s_in, kbuf, vbuf, cbuf, t_valid):
    nb = s_in.shape[0]
    sps = 8
    assert nb % sps == 0 and row0 % (sps * SEQ_PAD) == 0
    blk0 = row0 // (sps * SEQ_PAD)
    const3 = lambda bi: (0, 0, 0)
    const4 = lambda bi: (0, 0, 0, 0)
    step3 = lambda bi: (bi, 0, 0)
    step4 = lambda bi: (bi, 0, 0, 0)
    return pl.pallas_call(
        functools.partial(_mix_sample_kernel, layer=layer, sdec=sdec, t_valid=t_valid, seqs_per_step=sps),
        grid=(nb // sps,),
        in_specs=[
            pl.BlockSpec(memory_space=pltpu.SMEM),
            pl.BlockSpec((sps * SEQ_PAD, PROJ_WIDTH), lambda bi: (blk0 + bi, 0)),
            pl.BlockSpec((RET_HEADS, MIX_ROWS, CHUNK), const3),
            pl.BlockSpec((RET_HEADS, MIX_ROWS, CHUNK), const3),
            pl.BlockSpec((RET_HEADS, CHUNK, CHUNK), const3),
            pl.BlockSpec((1, ATT_Q_HEADS, MIX_ROWS, 2 * CHUNK), const4),
            pl.BlockSpec((1, CONV_WIDTH, CONV_CH), lambda bi: (layer, 0, 0)),
            pl.BlockSpec((sps, RET_HEADS, RET_DK, RET_DV), step4),
            pl.BlockSpec((sps, WINDOW, 128), step3),
            pl.BlockSpec((sps, WINDOW, 128), step3),
            pl.BlockSpec((sps, CONV_WIDTH - 1, CONV_CH), step3),
        ],
        out_specs=[
            pl.BlockSpec((sps * SEQ_PAD, D_MODEL), lambda bi: (bi, 0)),
            pl.BlockSpec((sps, RET_HEADS, RET_DK, RET_DV), step4),
            pl.BlockSpec((sps, WINDOW, 128), step3),
            pl.BlockSpec((sps, WINDOW, 128), step3),
            pl.BlockSpec((sps, CONV_WIDTH - 1, CONV_CH), step3),
        ],
        out_shape=[
            jax.ShapeDtypeStruct((nb * SEQ_PAD, D_MODEL), BF16),
            jax.ShapeDtypeStruct((nb, RET_HEADS, RET_DK, RET_DV), F32),
            jax.ShapeDtypeStruct((nb, WINDOW, 128), F32),
            jax.ShapeDtypeStruct((nb, WINDOW, 128), F32),
            jax.ShapeDtypeStruct((nb, CONV_WIDTH - 1, CONV_CH), F32),
        ],
        scratch_shapes=[
            pltpu.VMEM((sps, MIX_ROWS + 8, CONV_CH), F32),
            pltpu.VMEM((sps, 2, WINDOW + MIX_ROWS, 128), F32),
            pltpu.VMEM((sps, MIX_ROWS, D_MODEL), F32),
        ],
        compiler_params=_params("arbitrary"),
        name="mix_sample",
    )(attn_sinks, proj, *tables, conv_w, s_in, kbuf, vbuf, cbuf)


def kernel(x_prompt, x_sample, state_ret, cache_win_k, cache_win_v, state_conv, c_prompt, c_sample,
           w_in, w_out, conv_w, attn_sinks, norm_g, w_ada, b_ada, w_ff_gate, w_ff_up, w_ff_down):
    batch, seq, _ = x_prompt.shape
    nb, t_valid, _ = x_sample.shape
    w_buf = cache_win_k.shape[2]
    assert w_buf == WINDOW and seq % CHUNK == 0 and t_valid <= SEQ_PAD and nb + batch <= MOD_ROWS
    rows_p, rows_s = batch * seq, nb * SEQ_PAD
    rows = rows_p + rows_s

    c_all = jnp.concatenate([c_sample, c_prompt, jnp.zeros((MOD_ROWS - nb - batch, D_MODEL), F32)], axis=0)
    mod4 = _ada_call(c_all, w_ada, b_ada).reshape(DEPTH, MOD_ROWS, 1, N_MOD * D_MODEL)
    gains = norm_g.reshape(DEPTH * 4, 1, D_MODEL)

    tiles_out = tiles_down = _RowTiles(256, rows_p, rows_s, seq, nb)
    tm_stream = rows // 8
    assert rows % 8 == 0 and tm_stream % SEQ_PAD == 0
    tabs_p, sdec_p = _mixer_tables(CHUNK, CHUNK, True)
    tabs_s, sdec_s = _mixer_tables(MIX_ROWS, t_valid, False)

    xp = x_prompt.reshape(rows_p, D_MODEL)
    xs = jnp.pad(x_sample, ((0, 0), (0, SEQ_PAD - t_valid), (0, 0))).reshape(rows_s, D_MODEL)
    kbuf = cache_win_k.reshape(DEPTH, nb, w_buf, 128)
    vbuf = cache_win_v.reshape(DEPTH, nb, w_buf, 128)

    h = _prenorm_call(tiles_out, 0, xp, xs, gains, mod4)
    outs_p, outs_s = [], []
    for l in range(DEPTH):
        proj = _inproj_call(l, h, w_in, tm_stream, 1280)
        mixed_p, *st_p, w_out_b = _mix_prompt_call(l, proj, batch, seq, conv_w, attn_sinks, tabs_p, sdec_p, w_out)
        mixed_s, *st_s = _mix_sample_call(l, proj, rows_p, conv_w, attn_sinks, tabs_s, sdec_s,
                                          state_ret[l], kbuf[l], vbuf[l], state_conv[l], t_valid)
        mixed = (mixed_p.reshape(rows_p, D_MODEL), mixed_s)
        xp, xs, h2 = _resid_call("out_proj", tiles_out, 128, mixed, w_out_b, xp, xs, gains, mod4, l,
                                 1, MOD_G1, (l, 2, MOD_SC2, MOD_SH2))
        hid, w_down_b = _gateup_call(l, h2, w_ff_gate, w_ff_up, w_ff_down, tm_stream, 512)
        next_h = (l + 1, 0, MOD_SC1, MOD_SH1) if l + 1 < DEPTH else None
        res = _resid_call("ffn_down", tiles_down, 128, hid, w_down_b, xp, xs, gains, mod4, l,
                          3, MOD_G2, next_h)
        xp, xs = res[0], res[1]
        if next_h is not None:
            h = res[2]
        outs_p.append(st_p)
        outs_s.append(st_s)

    def stack(outs, idx, shape):
        return jnp.stack([o[idx] for o in outs]).reshape(shape)

    kv_shape_p = (DEPTH, batch, w_buf, ATT_KV_HEADS, ATT_HEAD_DIM)
    kv_shape_s = (DEPTH, nb, w_buf, ATT_KV_HEADS, ATT_HEAD_DIM)
    return (
        xp.reshape(batch, seq, D_MODEL),
        xs.reshape(nb, SEQ_PAD, D_MODEL)[:, :t_valid],
        stack(outs_p, 0, (DEPTH, batch, RET_HEADS, RET_DK, RET_DV)),
        stack(outs_s, 0, (DEPTH, nb, RET_HEADS, RET_DK, RET_DV)),
        stack(outs_p, 1, kv_shape_p),
        stack(outs_s, 1, kv_shape_s),
        stack(outs_p, 2, kv_shape_p),
        stack(outs_s, 2, kv_shape_s),
        stack(outs_p, 3, (DEPTH, batch, CONV_WIDTH - 1, CONV_CH)),
        stack(outs_s, 3, (DEPTH, nb, CONV_WIDTH - 1, CONV_CH)),
    )
```

```python
import functools
import math

import numpy as np
import jax
import jax.numpy as jnp
from jax import lax
from jax.experimental import pallas as pl
from jax.experimental.pallas import tpu as pltpu

F32 = jnp.float32
BF16 = jnp.bfloat16

D_MODEL = 2048
DEPTH = 4
RET_DK = 128
RET_DV = 128
RET_HEADS = 8
CHUNK = 128
ATT_HEAD_DIM = 64
ATT_Q_HEADS = 8
ATT_KV_HEADS = 2
ATT_GROUP = 4
WINDOW = 128
CONV_WIDTH = 3
CONV_CH = 512
D_FF = 5632
N_MOD = 6
EPS = 1e-6
NEG_INF = -1e30
PROJ_WIDTH = 6400
OFF_RQ, OFF_RK, OFF_RV, OFF_RG = 0, 1024, 2048, 3072
OFF_AQ, OFF_AK, OFF_AV = 4096, 4608, 4736
OFF_CB, OFF_CC, OFF_CH = 4864, 5376, 5888
MOD_SH1, MOD_SC1, MOD_G1, MOD_SH2, MOD_SC2, MOD_G2 = range(6)

SEQ_PAD = 8
MIX_ROWS = 16
MOD_ROWS = 48
VMEM_LIMIT_V7X = 56 * 1024 * 1024

LOG_GAMMA = [math.log1p(-(2.0 ** (-5.0 - h))) for h in range(RET_HEADS)]
ALIBI_SLOPES = [2.0 ** (-8.0 * (h + 1) / ATT_Q_HEADS) for h in range(ATT_Q_HEADS)]


def _silu(a):
    return a / (1.0 + jnp.exp(-a))


def _rms(x, gain):
    return x * lax.rsqrt(jnp.mean(x * x, axis=-1, keepdims=True) + EPS) * gain


def _norm_mod(x, gain, scale, shift):
    return _rms(x, gain) * (1.0 + scale) + shift


def _params(*sem):
    return pltpu.CompilerParams(dimension_semantics=sem, vmem_limit_bytes=VMEM_LIMIT_V7X)


def _ada_kernel(c_ref, w_ref, b_ref, o_ref):
    a = _silu(c_ref[...]).astype(BF16)
    o_ref[0] = jnp.dot(a, w_ref[0].astype(BF16), preferred_element_type=F32) + b_ref[0]


def _ada_call(c_all, w_ada, b_ada):
    tn = 1024
    n = N_MOD * D_MODEL
    return pl.pallas_call(
        _ada_kernel,
        grid=(DEPTH, n // tn),
        in_specs=[
            pl.BlockSpec((MOD_ROWS, D_MODEL), lambda l, j: (0, 0)),
            pl.BlockSpec((1, D_MODEL, tn), lambda l, j: (l, 0, j)),
            pl.BlockSpec((1, 1, tn), lambda l, j: (l, 0, j)),
        ],
        out_specs=pl.BlockSpec((1, MOD_ROWS, tn), lambda l, j: (l, 0, j)),
        out_shape=jax.ShapeDtypeStruct((DEPTH, MOD_ROWS, n), F32),
        compiler_params=_params("arbitrary", "arbitrary"),
        name="ada_mod",
    )(c_all, w_ada, b_ada.reshape(DEPTH, 1, n))


class _RowTiles:
    def __init__(self, tm, rows_prompt, rows_sample, seq_len, n_seq):
        assert rows_prompt % tm == 0 and rows_sample % tm == 0 and seq_len % tm == 0 and tm % SEQ_PAD == 0
        self.tm = tm
        self.n_prompt = rows_prompt // tm
        self.n_sample = rows_sample // tm
        self.n_tiles = self.n_prompt + self.n_sample
        self.tiles_per_seq = seq_len // tm
        self.seqs_per_tile = tm // SEQ_PAD
        self.prompt_mod_row0 = n_seq

    def all_rows(self, width):
        return pl.BlockSpec((self.tm, width), lambda i: (i, 0))

    def prompt_rows(self, width):
        last = self.n_prompt - 1
        return pl.BlockSpec((self.tm, width), lambda i: (jnp.minimum(i, last), 0))

    def sample_rows(self, width):
        first = self.n_prompt
        return pl.BlockSpec((self.tm, width), lambda i: (jnp.maximum(i - first, 0), 0))

    def prompt_mod(self, layer, chunk):
        last, per, row0 = self.n_prompt - 1, self.tiles_per_seq, self.prompt_mod_row0
        return pl.BlockSpec((1, 1, 1, D_MODEL), lambda i: (layer, row0 + jnp.minimum(i, last) // per, 0, chunk))

    def sample_mod(self, layer, chunk):
        first = self.n_prompt
        return pl.BlockSpec((1, self.seqs_per_tile, 1, D_MODEL),
                            lambda i: (layer, jnp.maximum(i - first, 0), 0, chunk))


def _gain_spec(layer, which):
    return pl.BlockSpec((1, 1, D_MODEL), lambda i: (layer * 4 + which, 0, 0))


def _const_weight_spec(layer, k, n):
    return pl.BlockSpec((1, k, n), lambda i: (layer, 0, 0), pipeline_mode=pl.Buffered(1))


def _for_each_group(n_prompt_tiles, fn):
    i = pl.program_id(0)

    @pl.when(i < n_prompt_tiles)
    def _():
        fn(False)

    @pl.when(i >= n_prompt_tiles)
    def _():
        fn(True)


def _sub_blocks(tm, sub, is_sample):
    for r in range(tm // sub):
        rows = slice(r * sub, (r + 1) * sub)
        if is_sample:
            g = sub // SEQ_PAD
            yield rows, (g, SEQ_PAD, D_MODEL), slice(r * g, (r + 1) * g)
        else:
            yield rows, (1, sub, D_MODEL), slice(0, 1)


def _prenorm_kernel(xp_ref, xs_ref, gain_ref, scp_ref, shp_ref, scs_ref, shs_ref, h_ref, *, tiles, sub):
    def run(is_sample):
        x_ref, sc_ref, sh_ref = (xs_ref, scs_ref, shs_ref) if is_sample else (xp_ref, scp_ref, shp_ref)
        for rows, shape3, seqs in _sub_blocks(tiles.tm, sub, is_sample):
            h = _norm_mod(x_ref[rows, :].reshape(shape3), gain_ref[...], sc_ref[0, seqs], sh_ref[0, seqs])
            h_ref[rows, :] = h.reshape(sub, D_MODEL).astype(BF16)

    _for_each_group(tiles.n_prompt, run)


def _prenorm_call(tiles, layer, xp, xs, gains, mod4):
    return pl.pallas_call(
        functools.partial(_prenorm_kernel, tiles=tiles, sub=128),
        grid=(tiles.n_tiles,),
        in_specs=[
            tiles.prompt_rows(D_MODEL), tiles.sample_rows(D_MODEL), _gain_spec(layer, 0),
            tiles.prompt_mod(layer, MOD_SC1), tiles.prompt_mod(layer, MOD_SH1),
            tiles.sample_mod(layer, MOD_SC1), tiles.sample_mod(layer, MOD_SH1),
        ],
        out_specs=tiles.all_rows(D_MODEL),
        out_shape=jax.ShapeDtypeStruct((tiles.n_tiles * tiles.tm, D_MODEL), BF16),
        compiler_params=_params("arbitrary"),
        name="prenorm",
    )(xp, xs, gains, mod4, mod4, mod4, mod4)


def _inproj_kernel(h_ref, w_ref, o_ref, wb_scr):
    @pl.when(pl.program_id(1) == 0)
    def _():
        wb_scr[...] = w_ref[0].astype(BF16)

    o_ref[...] = jnp.dot(h_ref[...], wb_scr[...], preferred_element_type=F32).astype(o_ref.dtype)


def _inproj_call(layer, h, w_in, tm, tn):
    rows = h.shape[0]
    return pl.pallas_call(
        _inproj_kernel,
        grid=(PROJ_WIDTH // tn, rows // tm),
        in_specs=[
            pl.BlockSpec((tm, D_MODEL), lambda j, i: (i, 0)),
            pl.BlockSpec((1, D_MODEL, tn), lambda j, i: (layer, 0, j)),
        ],
        out_specs=pl.BlockSpec((tm, tn), lambda j, i: (i, j)),
        out_shape=jax.ShapeDtypeStruct((rows, PROJ_WIDTH), BF16),
        scratch_shapes=[pltpu.VMEM((D_MODEL, tn), BF16)],
        compiler_params=_params("arbitrary", "arbitrary"),
        name="in_proj",
    )(h, w_in)


def _gateup_kernel(h_ref, wg_ref, wu_ref, wd_ref, o_ref, wdb_ref, wg_scr, wu_scr):
    @pl.when(pl.program_id(1) == 0)
    def _():
        wg_scr[...] = wg_ref[0].astype(BF16)
        wu_scr[...] = wu_ref[0].astype(BF16)

    wdb_ref[0] = wd_ref[0].astype(BF16)
    h = h_ref[...]
    a = jnp.dot(h, wg_scr[...], preferred_element_type=F32)
    b = jnp.dot(h, wu_scr[...], preferred_element_type=F32)
    o_ref[...] = (_silu(a) * b).astype(o_ref.dtype)


def _gateup_call(layer, h2, w_gate, w_up, w_down, tm, tn):
    rows = h2.shape[0]
    n_i = rows // tm
    wd_rows = tn // n_i
    assert tn % n_i == 0 and wd_rows % 16 == 0
    w_spec = pl.BlockSpec((1, D_MODEL, tn), lambda j, i: (layer, 0, j))
    return pl.pallas_call(
        _gateup_kernel,
        grid=(D_FF // tn, n_i),
        in_specs=[pl.BlockSpec((tm, D_MODEL), lambda j, i: (i, 0)), w_spec, w_spec,
                  pl.BlockSpec((1, wd_rows, D_MODEL), lambda j, i: (layer, j * n_i + i, 0))],
        out_specs=[pl.BlockSpec((tm, tn), lambda j, i: (i, j)),
                   pl.BlockSpec((1, wd_rows, D_MODEL), lambda j, i: (0, j * n_i + i, 0))],
        out_shape=[jax.ShapeDtypeStruct((rows, D_FF), BF16), jax.ShapeDtypeStruct((1, D_FF, D_MODEL), BF16)],
        scratch_shapes=[pltpu.VMEM((D_MODEL, tn), BF16), pltpu.VMEM((D_MODEL, tn), BF16)],
        compiler_params=_params("arbitrary", "arbitrary"),
        name="gate_up",
    )(h2, w_gate, w_up, w_down)


def _resid_kernel(*refs, tiles, sub, emit_h, a_split):
    refs = list(refs)
    ap_ref = refs.pop(0)
    as_ref = refs.pop(0) if a_split else ap_ref
    if emit_h:
        (w_ref, xp_ref, xs_ref, gain1_ref, gp_ref, gs_ref, gain2_ref, scp_ref, shp_ref, scs_ref, shs_ref,
         yp_ref, ys_ref, h_ref) = refs
    else:
        w_ref, xp_ref, xs_ref, gain1_ref, gp_ref, gs_ref, yp_ref, ys_ref = refs

    def run(is_sample):
        a_ref, x_ref, y_ref, g_ref = (as_ref, xs_ref, ys_ref, gs_ref) if is_sample else (ap_ref, xp_ref, yp_ref, gp_ref)
        for rows, shape3, seqs in _sub_blocks(tiles.tm, sub, is_sample):
            o = jnp.dot(a_ref[rows, :], w_ref[0], preferred_element_type=F32).reshape(shape3)
            gated_gain = gain1_ref[...] * g_ref[0, seqs]
            y = x_ref[rows, :].reshape(shape3) + o * lax.rsqrt(jnp.mean(o * o, axis=-1, keepdims=True) + EPS) * gated_gain
            y_ref[rows, :] = y.reshape(sub, D_MODEL)
            if emit_h:
                sc_ref, sh_ref = (scs_ref, shs_ref) if is_sample else (scp_ref, shp_ref)
                scaled_gain = gain2_ref[...] * (1.0 + sc_ref[0, seqs])
                h = y * lax.rsqrt(jnp.mean(y * y, axis=-1, keepdims=True) + EPS) * scaled_gain + sh_ref[0, seqs]
                h_ref[rows, :] = h.reshape(sub, D_MODEL).astype(BF16)

    _for_each_group(tiles.n_prompt, run)


def _resid_call(name, tiles, sub, a, w_b, xp, xs, gains, mod4, layer, gain1, gate_chunk, h_params):
    emit_h = h_params is not None
    a_split = isinstance(a, tuple)
    k_dim = w_b.shape[1]
    if a_split:
        in_specs, args = [tiles.prompt_rows(k_dim), tiles.sample_rows(k_dim)], list(a)
    else:
        in_specs, args = [tiles.all_rows(k_dim)], [a]
    in_specs += [
        _const_weight_spec(0, k_dim, D_MODEL),
        tiles.prompt_rows(D_MODEL), tiles.sample_rows(D_MODEL),
        _gain_spec(layer, gain1), tiles.prompt_mod(layer, gate_chunk), tiles.sample_mod(layer, gate_chunk),
    ]
    args += [w_b, xp, xs, gains, mod4, mod4]
    out_specs = [tiles.prompt_rows(D_MODEL), tiles.sample_rows(D_MODEL)]
    out_shape = [jax.ShapeDtypeStruct(xp.shape, F32), jax.ShapeDtypeStruct(xs.shape, F32)]
    if emit_h:
        hl, hg, hsc, hsh = h_params
        in_specs += [_gain_spec(hl, hg), tiles.prompt_mod(hl, hsc), tiles.prompt_mod(hl, hsh),
                     tiles.sample_mod(hl, hsc), tiles.sample_mod(hl, hsh)]
        args += [gains, mod4, mod4, mod4, mod4]
        out_specs.append(tiles.all_rows(D_MODEL))
        out_shape.append(jax.ShapeDtypeStruct((tiles.n_tiles * tiles.tm, D_MODEL), BF16))
    return pl.pallas_call(
        functools.partial(_resid_kernel, tiles=tiles, sub=sub, emit_h=emit_h, a_split=a_split),
        grid=(tiles.n_tiles,),
        in_specs=in_specs,
        out_specs=out_specs,
        out_shape=out_shape,
        compiler_params=_params("arbitrary"),
        name=name,
    )(*args)


def _mixer_tables(t_rows, t_valid, first_chunk_variant):
    qi = np.arange(t_rows)[:, None]
    kj = np.arange(2 * CHUNK)[None, :]
    dist = qi + CHUNK - kj
    allowed = (dist >= 0) & (dist < WINDOW)
    variants = [allowed & (kj >= CHUNK), allowed] if first_chunk_variant else [allowed]
    bias = np.stack([np.stack([np.where(ok, -slope * dist.astype(np.float64), NEG_INF) for slope in ALIBI_SLOPES])
                     for ok in variants])
    scale = RET_DK ** -0.5
    i = np.arange(t_rows)[:, None].astype(np.float64)
    j = np.arange(CHUNK)[None, :].astype(np.float64)
    jr = np.arange(CHUNK)[:, None].astype(np.float64)
    dtab = np.zeros((RET_HEADS, t_rows, CHUNK), np.float64)
    qtab = np.zeros((RET_HEADS, t_rows, CHUNK), np.float64)
    ktab = np.zeros((RET_HEADS, CHUNK, CHUNK), np.float64)
    sdec = []
    for h, lg in enumerate(LOG_GAMMA):
        causal = (i >= j) & (i < t_valid) & (j < t_valid)
        dtab[h] = np.where(causal, scale * np.exp(np.maximum(i - j, 0.0) * lg), 0.0)
        qtab[h] = np.where(i < t_valid, np.exp((i + 1.0) * lg), 0.0) * np.ones((1, CHUNK))
        ktab[h] = np.where(jr < t_valid, scale * np.exp(np.maximum(t_valid - 1.0 - jr, 0.0) * lg), 0.0) * np.ones((1, CHUNK))
        sdec.append(float(np.exp(t_valid * lg)))
    return tuple(jnp.asarray(t, F32) for t in (dtab, qtab, ktab, bias)), sdec


class _Chain:
    def __init__(self, proj, state, k_prev, v_prev, u_prev, mixed_ref, ubuf):
        self.proj, self.state, self.k_prev, self.v_prev, self.u_prev = proj, state, k_prev, v_prev, u_prev
        self.mixed_ref, self.ubuf = mixed_ref, ubuf


def _mixer_math(chains, tabs, bias_variant, sdec, conv_w, sinks):
    dtab_ref, qtab_ref, ktab_ref, bias_ref = tabs
    t_rows = chains[0].mixed_ref.shape[0]
    pad = CHUNK - t_rows
    nt = (((1,), (1,)), ((), ()))
    tn = (((0,), (0,)), ((), ()))
    att_base = RET_HEADS * RET_DV
    conv_base = att_base + ATT_Q_HEADS * ATT_HEAD_DIM
    att_scale = ATT_HEAD_DIM ** -0.5
    kv_lanes = [slice(g * ATT_HEAD_DIM, (g + 1) * ATT_HEAD_DIM) for g in range(ATT_KV_HEADS)]
    heads, att_heads = range(RET_HEADS), range(ATT_Q_HEADS)

    def pad_rows(a):
        if pad == 0:
            return a
        return jnp.concatenate([a, jnp.zeros((pad, a.shape[1]), a.dtype)], axis=0)

    for ch in chains:
        proj = ch.proj
        ch.k_cur = pad_rows(proj(OFF_AK, OFF_AK + 128))
        ch.v_cur = pad_rows(proj(OFF_AV, OFF_AV + 128))
        k_all = jnp.concatenate([ch.k_prev, ch.k_cur], axis=0)
        ch.v_all = jnp.concatenate([ch.v_prev, ch.v_cur], axis=0)
        rq = [proj(OFF_RQ + h * RET_DK, OFF_RQ + (h + 1) * RET_DK) for h in heads]
        rk = [pad_rows(proj(OFF_RK + h * RET_DK, OFF_RK + (h + 1) * RET_DK)) for h in heads]
        ch.rv = [pad_rows(proj(OFF_RV + h * RET_DV, OFF_RV + (h + 1) * RET_DV)) for h in heads]
        ch.ret_s = [lax.dot_general(rq[h], rk[h], nt, preferred_element_type=F32) for h in heads]
        ch.inter = [jnp.dot(rq[h], ch.state[h].astype(BF16), preferred_element_type=F32) for h in heads]
        ch.att_s = []
        for hh in att_heads:
            q = proj(OFF_AQ + hh * ATT_HEAD_DIM, OFF_AQ + (hh + 1) * ATT_HEAD_DIM) * att_scale
            ch.att_s.append(lax.dot_general(q, k_all[:, kv_lanes[hh // ATT_GROUP]], nt, preferred_element_type=F32))
        ch.new_state = []
        for h in heads:
            ks = (rk[h].astype(F32) * ktab_ref[h]).astype(BF16)
            kv = lax.dot_general(ks, ch.rv[h], tn, preferred_element_type=F32)
            ch.new_state.append(sdec[h] * ch.state[h] + kv)

    for ch in chains:
        ch.ret_p = [(ch.ret_s[h] * dtab_ref[h]).astype(BF16) for h in heads]
        ch.att_e, ch.att_den = [], []
        for hh in att_heads:
            s = ch.att_s[hh] + bias_ref[bias_variant, hh]
            sink = sinks[hh]
            m = jnp.maximum(jnp.max(s, axis=-1, keepdims=True), sink)
            e = jnp.exp(s - m)
            ch.att_den.append(jnp.sum(e, axis=-1, keepdims=True) + jnp.exp(sink - m))
            ch.att_e.append(e.astype(BF16))
        proj, ubuf = ch.proj, ch.ubuf
        gate_b = proj(OFF_CB, OFF_CB + CONV_CH).astype(F32)
        u = proj(OFF_CC, OFF_CC + CONV_CH).astype(F32) * proj(OFF_CH, OFF_CH + CONV_CH).astype(F32)
        ubuf[6:8, :] = ch.u_prev
        ubuf[8:8 + t_rows, :] = u
        y = conv_w[0:1, :] * ubuf[6:6 + t_rows, :] + conv_w[1:2, :] * ubuf[7:7 + t_rows, :] + conv_w[2:3, :] * u
        ch.mixed_ref[:, conv_base:conv_base + CONV_CH] = (gate_b * y).astype(ch.mixed_ref.dtype)

    for ch in chains:
        ch.intra = [jnp.dot(ch.ret_p[h], ch.rv[h], preferred_element_type=F32) for h in heads]
        ch.att_o = [jnp.dot(ch.att_e[hh], ch.v_all[:, kv_lanes[hh // ATT_GROUP]], preferred_element_type=F32)
                    for hh in att_heads]

    for ch in chains:
        for h in heads:
            gate = ch.proj(OFF_RG + h * RET_DV, OFF_RG + (h + 1) * RET_DV).astype(F32)
            ry = ch.intra[h] + ch.inter[h] * qtab_ref[h]
            ryn = ry * lax.rsqrt(jnp.mean(ry * ry, axis=-1, keepdims=True) + EPS)
            ch.mixed_ref[:, h * RET_DV:(h + 1) * RET_DV] = (_silu(gate) * ryn).astype(ch.mixed_ref.dtype)
        for hh in att_heads:
            c0 = att_base + hh * ATT_HEAD_DIM
            ch.mixed_ref[:, c0:c0 + ATT_HEAD_DIM] = (ch.att_o[hh] / ch.att_den[hh]).astype(ch.mixed_ref.dtype)
    return [(ch.new_state, ch.k_cur, ch.v_cur) for ch in chains]


def _mix_prompt_kernel(sink_ref, *refs, layer, sdec, batch, n_carried):
    proj_refs, refs = refs[:batch], refs[batch:]
    (dtab_ref, qtab_ref, ktab_ref, bias_ref, convw_ref, wout_ref), refs = refs[:6], refs[6 + n_carried:]
    (mixed_ref, sret_ref, knew_ref, vnew_ref, cnew_ref,
     woutb_ref, s_scr, kprev_scr, vprev_scr, uprev_scr, ubuf) = refs
    c = pl.program_id(0)
    woutb_ref[0] = wout_ref[0].astype(BF16)

    @pl.when(c == 0)
    def _():
        s_scr[...] = jnp.zeros_like(s_scr)
        kprev_scr[...] = jnp.zeros_like(kprev_scr)
        vprev_scr[...] = jnp.zeros_like(vprev_scr)
        uprev_scr[...] = jnp.zeros_like(uprev_scr)

    sinks = [sink_ref[layer, hh] for hh in range(ATT_Q_HEADS)]
    bias_variant = jnp.where(c == 0, 0, 1)
    last = c == pl.num_programs(0) - 1
    chains = [_Chain(functools.partial(lambda ref, lo, hi: ref[:, lo:hi], proj_refs[b]),
                     [s_scr[b, h] for h in range(RET_HEADS)], kprev_scr[b], vprev_scr[b], uprev_scr[b],
                     mixed_ref.at[b], ubuf.at[b]) for b in range(batch)]
    results = [_mixer_math([ch], (dtab_ref, qtab_ref, ktab_ref, bias_ref), bias_variant, sdec, convw_ref[0], sinks)[0]
               for ch in chains]
    for b, (new_state, k_cur, v_cur) in enumerate(results):
        for h in range(RET_HEADS):
            s_scr[b, h] = new_state[h]
        kprev_scr[b] = k_cur
        vprev_scr[b] = v_cur
        uprev_scr[b] = ubuf[b, CHUNK + 6:CHUNK + 8, :]

        @pl.when(last)
        def _():
            for h in range(RET_HEADS):
                sret_ref[0, b, h] = new_state[h]
            knew_ref[0, b] = k_cur.astype(F32)
            vnew_ref[0, b] = v_cur.astype(F32)
            cnew_ref[0, b] = ubuf[b, CHUNK + 6:CHUNK + 8, :]


STATE_TAILS = [(RET_HEADS, RET_DK, RET_DV), (WINDOW, 128), (WINDOW, 128), (CONV_WIDTH - 1, CONV_CH)]


def _state_specs(layer, seqs_per_block, seq_block):
    return [pl.BlockSpec((1, seqs_per_block) + tail,
                         functools.partial(lambda z, i: (layer, seq_block(i)) + z, (0,) * len(tail)))
            for tail in STATE_TAILS]


def _state_shapes(n_seq):
    return [jax.ShapeDtypeStruct((DEPTH, n_seq) + tail, F32) for tail in STATE_TAILS]


def _carried_args(carried, n_fixed_inputs):
    if carried is None:
        return [], [], {}
    specs = [pl.BlockSpec(memory_space=pl.ANY)] * len(carried)
    return specs, list(carried), {n_fixed_inputs + k: 1 + k for k in range(len(carried))}


def _mix_prompt_call(layer, proj, batch, seq, conv_w, attn_sinks, tables, sdec, w_out, carried):
    n_chunks = seq // CHUNK
    wrows = D_MODEL // n_chunks
    assert D_MODEL % n_chunks == 0 and wrows % 16 == 0
    const3 = lambda ci: (0, 0, 0)
    const4 = lambda ci: (0, 0, 0, 0)
    proj_specs = [pl.BlockSpec((CHUNK, PROJ_WIDTH), functools.partial(lambda b, ci: (b * n_chunks + ci, 0), b))
                  for b in range(batch)]
    carried_specs, carried_args, aliases = _carried_args(carried, 1 + batch + 6)
    return pl.pallas_call(
        functools.partial(_mix_prompt_kernel, layer=layer, sdec=sdec, batch=batch, n_carried=len(carried_args)),
        grid=(n_chunks,),
        in_specs=[pl.BlockSpec(memory_space=pltpu.SMEM)] + proj_specs + [
            pl.BlockSpec((RET_HEADS, CHUNK, CHUNK), const3),
            pl.BlockSpec((RET_HEADS, CHUNK, CHUNK), const3),
            pl.BlockSpec((RET_HEADS, CHUNK, CHUNK), const3),
            pl.BlockSpec((2, ATT_Q_HEADS, CHUNK, 2 * CHUNK), const4),
            pl.BlockSpec((1, CONV_WIDTH, CONV_CH), lambda ci: (layer, 0, 0)),
            pl.BlockSpec((1, wrows, D_MODEL), lambda ci: (layer, ci, 0)),
        ] + carried_specs,
        out_specs=[pl.BlockSpec((batch, CHUNK, D_MODEL), lambda ci: (0, ci, 0))]
        + _state_specs(layer, batch, lambda ci: 0)
        + [pl.BlockSpec((1, wrows, D_MODEL), lambda ci: (0, ci, 0))],
        out_shape=[jax.ShapeDtypeStruct((batch, seq, D_MODEL), BF16)] + _state_shapes(batch)
        + [jax.ShapeDtypeStruct((1, D_MODEL, D_MODEL), BF16)],
        input_output_aliases=aliases,
        scratch_shapes=[
            pltpu.VMEM((batch, RET_HEADS, RET_DK, RET_DV), F32),
            pltpu.VMEM((batch, CHUNK, 128), BF16),
            pltpu.VMEM((batch, CHUNK, 128), BF16),
            pltpu.VMEM((batch, CONV_WIDTH - 1, CONV_CH), F32),
            pltpu.VMEM((batch, CHUNK + 8, CONV_CH), F32),
        ],
        compiler_params=_params("arbitrary"),
        name="mix_prompt",
    )(attn_sinks, *([proj] * batch), *tables, conv_w, w_out, *carried_args)


def _mix_sample_kernel(sink_ref, proj_ref, dtab_ref, qtab_ref, ktab_ref, bias_ref, convw_ref,
                       sin_ref, kbuf_ref, vbuf_ref, cbuf_ref, *refs, layer, sdec, t_valid, seqs_per_step, n_carried):
    assert 2 * SEQ_PAD == MIX_ROWS and seqs_per_step % 2 == 0
    mixed_ref, sret_ref, knew_ref, vnew_ref, cnew_ref, ubuf, kvbuf, mix_scr = refs[n_carried:]
    sin_ref, kbuf_ref, vbuf_ref, cbuf_ref = sin_ref.at[0], kbuf_ref.at[0], vbuf_ref.at[0], cbuf_ref.at[0]
    sret_ref, knew_ref, vnew_ref, cnew_ref = sret_ref.at[0], knew_ref.at[0], vnew_ref.at[0], cnew_ref.at[0]

    def seq_proj(s, lo, hi):
        pair = proj_ref[(s // 2) * MIX_ROWS:(s // 2 + 1) * MIX_ROWS, lo:hi].astype(F32)
        own = pair[(s % 2) * SEQ_PAD:(s % 2 + 1) * SEQ_PAD]
        return jnp.concatenate([own, jnp.zeros((MIX_ROWS - SEQ_PAD, hi - lo), F32)], axis=0).astype(BF16)

    sinks = [sink_ref[layer, hh] for hh in range(ATT_Q_HEADS)]
    chains = [_Chain(functools.partial(seq_proj, s), [sin_ref[s, h] for h in range(RET_HEADS)],
                     kbuf_ref[s].astype(BF16), vbuf_ref[s].astype(BF16), cbuf_ref[s], mix_scr.at[s], ubuf.at[s])
              for s in range(seqs_per_step)]
    results = _mixer_math(chains, (dtab_ref, qtab_ref, ktab_ref, bias_ref), 0, sdec, convw_ref[0], sinks)
    for s, (new_state, _, _) in enumerate(results):
        for h in range(RET_HEADS):
            sret_ref[s, h] = new_state[h]
        for j, (src_ref, off, dst_ref) in enumerate(((kbuf_ref, OFF_AK, knew_ref), (vbuf_ref, OFF_AV, vnew_ref))):
            kvbuf[s, j, 0:WINDOW, :] = src_ref[s]
            kvbuf[s, j, WINDOW:WINDOW + MIX_ROWS, :] = seq_proj(s, off, off + 128).astype(F32)
            dst_ref[s] = kvbuf[s, j, t_valid:t_valid + WINDOW, :]
        cnew_ref[s] = ubuf[s, 8 + t_valid - 2:8 + t_valid, :]
    for p in range(seqs_per_step // 2):
        pair = jnp.concatenate([mix_scr[2 * p, 0:SEQ_PAD, :], mix_scr[2 * p + 1, 0:SEQ_PAD, :]], axis=0)
        mixed_ref[p * MIX_ROWS:(p + 1) * MIX_ROWS, :] = pair.astype(mixed_ref.dtype)


def _mix_sample_call(layer, proj, row0, conv_w, attn_sinks, tables, sdec, states_in, t_valid, carried):
    nb = states_in[0].shape[1]
    sps = 8
    assert nb % sps == 0 and row0 % (sps * SEQ_PAD) == 0
    blk0 = row0 // (sps * SEQ_PAD)
    const3 = lambda bi: (0, 0, 0)
    const4 = lambda bi: (0, 0, 0, 0)
    carried_specs, carried_args, aliases = _carried_args(carried, 11)
    return pl.pallas_call(
        functools.partial(_mix_sample_kernel, layer=layer, sdec=sdec, t_valid=t_valid, seqs_per_step=sps,
                          n_carried=len(carried_args)),
        grid=(nb // sps,),
        in_specs=[
            pl.BlockSpec(memory_space=pltpu.SMEM),
            pl.BlockSpec((sps * SEQ_PAD, PROJ_WIDTH), lambda bi: (blk0 + bi, 0)),
            pl.BlockSpec((RET_HEADS, MIX_ROWS, CHUNK), const3),
            pl.BlockSpec((RET_HEADS, MIX_ROWS, CHUNK), const3),
            pl.BlockSpec((RET_HEADS, CHUNK, CHUNK), const3),
            pl.BlockSpec((1, ATT_Q_HEADS, MIX_ROWS, 2 * CHUNK), const4),
            pl.BlockSpec((1, CONV_WIDTH, CONV_CH), lambda bi: (layer, 0, 0)),
        ] + _state_specs(layer, sps, lambda bi: bi) + carried_specs,
        out_specs=[pl.BlockSpec((sps * SEQ_PAD, D_MODEL), lambda bi: (bi, 0))]
        + _state_specs(layer, sps, lambda bi: bi),
        out_shape=[jax.ShapeDtypeStruct((nb * SEQ_PAD, D_MODEL), BF16)] + _state_shapes(nb),
        input_output_aliases=aliases,
        scratch_shapes=[
            pltpu.VMEM((sps, MIX_ROWS + 8, CONV_CH), F32),
            pltpu.VMEM((sps, 2, WINDOW + MIX_ROWS, 128), F32),
            pltpu.VMEM((sps, MIX_ROWS, D_MODEL), F32),
        ],
        compiler_params=_params("arbitrary"),
        name="mix_sample",
    )(attn_sinks, proj, *tables, conv_w, *states_in, *carried_args)


def kernel(x_prompt, x_sample, state_ret, cache_win_k, cache_win_v, state_conv, c_prompt, c_sample,
           w_in, w_out, conv_w, attn_sinks, norm_g, w_ada, b_ada, w_ff_gate, w_ff_up, w_ff_down):
    batch, seq, _ = x_prompt.shape
    nb, t_valid, _ = x_sample.shape
    w_buf = cache_win_k.shape[2]
    assert w_buf == WINDOW and seq % CHUNK == 0 and t_valid <= SEQ_PAD and nb + batch <= MOD_ROWS
    rows_p, rows_s = batch * seq, nb * SEQ_PAD
    rows = rows_p + rows_s

    c_all = jnp.concatenate([c_sample, c_prompt, jnp.zeros((MOD_ROWS - nb - batch, D_MODEL), F32)], axis=0)
    mod4 = _ada_call(c_all, w_ada, b_ada).reshape(DEPTH, MOD_ROWS, 1, N_MOD * D_MODEL)
    gains = norm_g.reshape(DEPTH * 4, 1, D_MODEL)

    tiles_out = tiles_down = _RowTiles(256, rows_p, rows_s, seq, nb)
    tm_stream = rows // 8
    assert rows % 8 == 0 and tm_stream % SEQ_PAD == 0
    tabs_p, sdec_p = _mixer_tables(CHUNK, CHUNK, True)
    tabs_s, sdec_s = _mixer_tables(MIX_ROWS, t_valid, False)

    xp = x_prompt.reshape(rows_p, D_MODEL)
    xs = jnp.pad(x_sample, ((0, 0), (0, SEQ_PAD - t_valid), (0, 0))).reshape(rows_s, D_MODEL)
    states_in = (state_ret, cache_win_k.reshape(DEPTH, nb, w_buf, 128), cache_win_v.reshape(DEPTH, nb, w_buf, 128),
                 state_conv)

    h = _prenorm_call(tiles_out, 0, xp, xs, gains, mod4)
    st_p = st_s = None
    for l in range(DEPTH):
        proj = _inproj_call(l, h, w_in, tm_stream, 1280)
        mixed_p, *st_p, w_out_b = _mix_prompt_call(l, proj, batch, seq, conv_w, attn_sinks, tabs_p, sdec_p, w_out,
                                                   st_p)
        mixed_s, *st_s = _mix_sample_call(l, proj, rows_p, conv_w, attn_sinks, tabs_s, sdec_s, states_in, t_valid,
                                          st_s)
        mixed = (mixed_p.reshape(rows_p, D_MODEL), mixed_s)
        xp, xs, h2 = _resid_call("out_proj", tiles_out, 128, mixed, w_out_b, xp, xs, gains, mod4, l,
                                 1, MOD_G1, (l, 2, MOD_SC2, MOD_SH2))
        hid, w_down_b = _gateup_call(l, h2, w_ff_gate, w_ff_up, w_ff_down, tm_stream, 512)
        next_h = (l + 1, 0, MOD_SC1, MOD_SH1) if l + 1 < DEPTH else None
        res = _resid_call("ffn_down", tiles_down, 128, hid, w_down_b, xp, xs, gains, mod4, l,
                          3, MOD_G2, next_h)
        xp, xs = res[0], res[1]
        if next_h is not None:
            h = res[2]

    kv_shape_p = (DEPTH, batch, w_buf, ATT_KV_HEADS, ATT_HEAD_DIM)
    kv_shape_s = (DEPTH, nb, w_buf, ATT_KV_HEADS, ATT_HEAD_DIM)
    return (
        xp.reshape(batch, seq, D_MODEL),
        xs.reshape(nb, SEQ_PAD, D_MODEL)[:, :t_valid],
        st_p[0], st_s[0],
        st_p[1].reshape(kv_shape_p), st_s[1].reshape(kv_shape_s),
        st_p[2].reshape(kv_shape_p), st_s[2].reshape(kv_shape_s),
        st_p[3], st_s[3],
    )
```

```python
import functools
import math

import numpy as np
import jax
import jax.numpy as jnp
from jax import lax
from jax.experimental import pallas as pl
from jax.experimental.pallas import tpu as pltpu

F32 = jnp.float32
BF16 = jnp.bfloat16

D_MODEL = 2048
DEPTH = 4
RET_DK = 128
RET_DV = 128
RET_HEADS = 8
CHUNK = 128
ATT_HEAD_DIM = 64
ATT_Q_HEADS = 8
ATT_KV_HEADS = 2
ATT_GROUP = 4
WINDOW = 128
CONV_WIDTH = 3
CONV_CH = 512
D_FF = 5632
N_MOD = 6
EPS = 1e-6
NEG_INF = -1e30
PROJ_WIDTH = 6400
OFF_RQ, OFF_RK, OFF_RV, OFF_RG = 0, 1024, 2048, 3072
OFF_AQ, OFF_AK, OFF_AV = 4096, 4608, 4736
OFF_CB, OFF_CC, OFF_CH = 4864, 5376, 5888
MOD_SH1, MOD_SC1, MOD_G1, MOD_SH2, MOD_SC2, MOD_G2 = range(6)

SEQ_PAD = 8
MIX_ROWS = 16
MOD_ROWS = 48
VMEM_LIMIT_V7X = 56 * 1024 * 1024

LOG_GAMMA = [math.log1p(-(2.0 ** (-5.0 - h))) for h in range(RET_HEADS)]
ALIBI_SLOPES = [2.0 ** (-8.0 * (h + 1) / ATT_Q_HEADS) for h in range(ATT_Q_HEADS)]


def _silu(a):
    return a / (1.0 + jnp.exp(-a))


def _rms(x, gain):
    return x * lax.rsqrt(jnp.mean(x * x, axis=-1, keepdims=True) + EPS) * gain


def _norm_mod(x, gain, scale, shift):
    return _rms(x, gain) * (1.0 + scale) + shift


def _params(*sem):
    return pltpu.CompilerParams(dimension_semantics=sem, vmem_limit_bytes=VMEM_LIMIT_V7X)


def _ada_kernel(c_ref, w_ref, b_ref, o_ref):
    a = _silu(c_ref[...]).astype(BF16)
    o_ref[0] = jnp.dot(a, w_ref[0].astype(BF16), preferred_element_type=F32) + b_ref[0]


def _ada_call(c_all, w_ada, b_ada):
    tn = 1024
    n = N_MOD * D_MODEL
    return pl.pallas_call(
        _ada_kernel,
        grid=(DEPTH, n // tn),
        in_specs=[
            pl.BlockSpec((MOD_ROWS, D_MODEL), lambda l, j: (0, 0)),
            pl.BlockSpec((1, D_MODEL, tn), lambda l, j: (l, 0, j)),
            pl.BlockSpec((1, 1, tn), lambda l, j: (l, 0, j)),
        ],
        out_specs=pl.BlockSpec((1, MOD_ROWS, tn), lambda l, j: (l, 0, j)),
        out_shape=jax.ShapeDtypeStruct((DEPTH, MOD_ROWS, n), F32),
        compiler_params=_params("arbitrary", "arbitrary"),
        name="ada_mod",
    )(c_all, w_ada, b_ada.reshape(DEPTH, 1, n))


class _RowTiles:
    def __init__(self, tm, rows_prompt, rows_sample, seq_len, n_seq):
        assert rows_prompt % tm == 0 and rows_sample % tm == 0 and seq_len % tm == 0 and tm % SEQ_PAD == 0
        self.tm = tm
        self.n_prompt = rows_prompt // tm
        self.n_sample = rows_sample // tm
        self.n_tiles = self.n_prompt + self.n_sample
        self.tiles_per_seq = seq_len // tm
        self.seqs_per_tile = tm // SEQ_PAD
        self.prompt_mod_row0 = n_seq

    def all_rows(self, width):
        return pl.BlockSpec((self.tm, width), lambda i: (i, 0))

    def prompt_rows(self, width):
        last = self.n_prompt - 1
        return pl.BlockSpec((self.tm, width), lambda i: (jnp.minimum(i, last), 0))

    def sample_rows(self, width):
        first = self.n_prompt
        return pl.BlockSpec((self.tm, width), lambda i: (jnp.maximum(i - first, 0), 0))

    def prompt_mod(self, layer, chunk):
        last, per, row0 = self.n_prompt - 1, self.tiles_per_seq, self.prompt_mod_row0
        return pl.BlockSpec((1, 1, 1, D_MODEL), lambda i: (layer, row0 + jnp.minimum(i, last) // per, 0, chunk))

    def sample_mod(self, layer, chunk):
        first = self.n_prompt
        return pl.BlockSpec((1, self.seqs_per_tile, 1, D_MODEL),
                            lambda i: (layer, jnp.maximum(i - first, 0), 0, chunk))


def _gain_spec(layer, which):
    return pl.BlockSpec((1, 1, D_MODEL), lambda i: (layer * 4 + which, 0, 0))


def _const_weight_spec(layer, k, n):
    return pl.BlockSpec((1, k, n), lambda i: (layer, 0, 0), pipeline_mode=pl.Buffered(1))


def _for_each_group(n_prompt_tiles, fn):
    i = pl.program_id(0)

    @pl.when(i < n_prompt_tiles)
    def _():
        fn(False)

    @pl.when(i >= n_prompt_tiles)
    def _():
        fn(True)


def _sub_blocks(tm, sub, is_sample):
    for r in range(tm // sub):
        rows = slice(r * sub, (r + 1) * sub)
        if is_sample:
            g = sub // SEQ_PAD
            yield rows, (g, SEQ_PAD, D_MODEL), slice(r * g, (r + 1) * g)
        else:
            yield rows, (1, sub, D_MODEL), slice(0, 1)


def _prenorm_kernel(xp_ref, xs_ref, gain_ref, scp_ref, shp_ref, scs_ref, shs_ref, h_ref, *, tiles, sub):
    def run(is_sample):
        x_ref, sc_ref, sh_ref = (xs_ref, scs_ref, shs_ref) if is_sample else (xp_ref, scp_ref, shp_ref)
        for rows, shape3, seqs in _sub_blocks(tiles.tm, sub, is_sample):
            h = _norm_mod(x_ref[rows, :].reshape(shape3), gain_ref[...], sc_ref[0, seqs], sh_ref[0, seqs])
            h_ref[rows, :] = h.reshape(sub, D_MODEL).astype(BF16)

    _for_each_group(tiles.n_prompt, run)


def _prenorm_call(tiles, layer, xp, xs, gains, mod4):
    return pl.pallas_call(
        functools.partial(_prenorm_kernel, tiles=tiles, sub=128),
        grid=(tiles.n_tiles,),
        in_specs=[
            tiles.prompt_rows(D_MODEL), tiles.sample_rows(D_MODEL), _gain_spec(layer, 0),
            tiles.prompt_mod(layer, MOD_SC1), tiles.prompt_mod(layer, MOD_SH1),
            tiles.sample_mod(layer, MOD_SC1), tiles.sample_mod(layer, MOD_SH1),
        ],
        out_specs=tiles.all_rows(D_MODEL),
        out_shape=jax.ShapeDtypeStruct((tiles.n_tiles * tiles.tm, D_MODEL), BF16),
        compiler_params=_params("arbitrary"),
        name="prenorm",
    )(xp, xs, gains, mod4, mod4, mod4, mod4)


def _inproj_kernel(h_ref, w_ref, o_ref, wb_scr):
    @pl.when(pl.program_id(1) == 0)
    def _():
        wb_scr[...] = w_ref[0].astype(BF16)

    o_ref[...] = jnp.dot(h_ref[...], wb_scr[...], preferred_element_type=F32).astype(o_ref.dtype)


def _inproj_call(layer, h, w_in, tm, tn):
    rows = h.shape[0]
    return pl.pallas_call(
        _inproj_kernel,
        grid=(PROJ_WIDTH // tn, rows // tm),
        in_specs=[
            pl.BlockSpec((tm, D_MODEL), lambda j, i: (i, 0)),
            pl.BlockSpec((1, D_MODEL, tn), lambda j, i: (layer, 0, j)),
        ],
        out_specs=pl.BlockSpec((tm, tn), lambda j, i: (i, j)),
        out_shape=jax.ShapeDtypeStruct((rows, PROJ_WIDTH), BF16),
        scratch_shapes=[pltpu.VMEM((D_MODEL, tn), BF16)],
        compiler_params=_params("arbitrary", "arbitrary"),
        name="in_proj",
    )(h, w_in)


def _gateup_kernel(h_ref, wg_ref, wu_ref, wd_ref, o_ref, wdb_ref, wg_scr, wu_scr):
    @pl.when(pl.program_id(1) == 0)
    def _():
        wg_scr[...] = wg_ref[0].astype(BF16)
        wu_scr[...] = wu_ref[0].astype(BF16)

    wdb_ref[0] = wd_ref[0].astype(BF16)
    h = h_ref[...]
    a = jnp.dot(h, wg_scr[...], preferred_element_type=F32)
    b = jnp.dot(h, wu_scr[...], preferred_element_type=F32)
    o_ref[...] = (_silu(a) * b).astype(o_ref.dtype)


def _gateup_call(layer, h2, w_gate, w_up, w_down, tm, tn):
    rows = h2.shape[0]
    n_i = rows // tm
    n_spread = max(d for d in range(1, n_i + 1) if tn % d == 0 and (tn // d) % 16 == 0)
    wd_rows = tn // n_spread
    wd_block = lambda j, i: j * n_spread + jnp.minimum(i, n_spread - 1)
    w_spec = pl.BlockSpec((1, D_MODEL, tn), lambda j, i: (layer, 0, j))
    return pl.pallas_call(
        _gateup_kernel,
        grid=(D_FF // tn, n_i),
        in_specs=[pl.BlockSpec((tm, D_MODEL), lambda j, i: (i, 0)), w_spec, w_spec,
                  pl.BlockSpec((1, wd_rows, D_MODEL), lambda j, i: (layer, wd_block(j, i), 0))],
        out_specs=[pl.BlockSpec((tm, tn), lambda j, i: (i, j)),
                   pl.BlockSpec((1, wd_rows, D_MODEL), lambda j, i: (0, wd_block(j, i), 0))],
        out_shape=[jax.ShapeDtypeStruct((rows, D_FF), BF16), jax.ShapeDtypeStruct((1, D_FF, D_MODEL), BF16)],
        scratch_shapes=[pltpu.VMEM((D_MODEL, tn), BF16), pltpu.VMEM((D_MODEL, tn), BF16)],
        compiler_params=_params("arbitrary", "arbitrary"),
        name="gate_up",
    )(h2, w_gate, w_up, w_down)


def _resid_kernel(*refs, tiles, sub, emit_h, a_split):
    refs = list(refs)
    ap_ref = refs.pop(0)
    as_ref = refs.pop(0) if a_split else ap_ref
    if emit_h:
        (w_ref, xp_ref, xs_ref, gain1_ref, gp_ref, gs_ref, gain2_ref, scp_ref, shp_ref, scs_ref, shs_ref,
         yp_ref, ys_ref, h_ref) = refs
    else:
        w_ref, xp_ref, xs_ref, gain1_ref, gp_ref, gs_ref, yp_ref, ys_ref = refs

    def run(is_sample):
        a_ref, x_ref, y_ref, g_ref = (as_ref, xs_ref, ys_ref, gs_ref) if is_sample else (ap_ref, xp_ref, yp_ref, gp_ref)
        for rows, shape3, seqs in _sub_blocks(tiles.tm, sub, is_sample):
            o = jnp.dot(a_ref[rows, :], w_ref[0], preferred_element_type=F32).reshape(shape3)
            gated_gain = gain1_ref[...] * g_ref[0, seqs]
            y = x_ref[rows, :].reshape(shape3) + o * lax.rsqrt(jnp.mean(o * o, axis=-1, keepdims=True) + EPS) * gated_gain
            y_ref[rows, :] = y.reshape(sub, D_MODEL)
            if emit_h:
                sc_ref, sh_ref = (scs_ref, shs_ref) if is_sample else (scp_ref, shp_ref)
                scaled_gain = gain2_ref[...] * (1.0 + sc_ref[0, seqs])
                h = y * lax.rsqrt(jnp.mean(y * y, axis=-1, keepdims=True) + EPS) * scaled_gain + sh_ref[0, seqs]
                h_ref[rows, :] = h.reshape(sub, D_MODEL).astype(BF16)

    _for_each_group(tiles.n_prompt, run)


def _resid_call(name, tiles, sub, a, w_b, xp, xs, gains, mod4, layer, gain1, gate_chunk, h_params):
    emit_h = h_params is not None
    a_split = isinstance(a, tuple)
    k_dim = w_b.shape[1]
    if a_split:
        in_specs, args = [tiles.prompt_rows(k_dim), tiles.sample_rows(k_dim)], list(a)
    else:
        in_specs, args = [tiles.all_rows(k_dim)], [a]
    in_specs += [
        _const_weight_spec(0, k_dim, D_MODEL),
        tiles.prompt_rows(D_MODEL), tiles.sample_rows(D_MODEL),
        _gain_spec(layer, gain1), tiles.prompt_mod(layer, gate_chunk), tiles.sample_mod(layer, gate_chunk),
    ]
    args += [w_b, xp, xs, gains, mod4, mod4]
    out_specs = [tiles.prompt_rows(D_MODEL), tiles.sample_rows(D_MODEL)]
    out_shape = [jax.ShapeDtypeStruct(xp.shape, F32), jax.ShapeDtypeStruct(xs.shape, F32)]
    if emit_h:
        hl, hg, hsc, hsh = h_params
        in_specs += [_gain_spec(hl, hg), tiles.prompt_mod(hl, hsc), tiles.prompt_mod(hl, hsh),
                     tiles.sample_mod(hl, hsc), tiles.sample_mod(hl, hsh)]
        args += [gains, mod4, mod4, mod4, mod4]
        out_specs.append(tiles.all_rows(D_MODEL))
        out_shape.append(jax.ShapeDtypeStruct((tiles.n_tiles * tiles.tm, D_MODEL), BF16))
    return pl.pallas_call(
        functools.partial(_resid_kernel, tiles=tiles, sub=sub, emit_h=emit_h, a_split=a_split),
        grid=(tiles.n_tiles,),
        in_specs=in_specs,
        out_specs=out_specs,
        out_shape=out_shape,
        compiler_params=_params("arbitrary"),
        name=name,
    )(*args)


def _mixer_tables(t_rows, t_valid, first_chunk_variant):
    qi = np.arange(t_rows)[:, None]
    kj = np.arange(2 * CHUNK)[None, :]
    dist = qi + CHUNK - kj
    allowed = (dist >= 0) & (dist < WINDOW)
    variants = [allowed & (kj >= CHUNK), allowed] if first_chunk_variant else [allowed]
    bias = np.stack([np.stack([np.where(ok, -slope * dist.astype(np.float64), NEG_INF) for slope in ALIBI_SLOPES])
                     for ok in variants])
    scale = RET_DK ** -0.5
    i = np.arange(t_rows)[:, None].astype(np.float64)
    j = np.arange(CHUNK)[None, :].astype(np.float64)
    jr = np.arange(CHUNK)[:, None].astype(np.float64)
    dtab = np.zeros((RET_HEADS, t_rows, CHUNK), np.float64)
    qtab = np.zeros((RET_HEADS, t_rows, CHUNK), np.float64)
    ktab = np.zeros((RET_HEADS, CHUNK, CHUNK), np.float64)
    sdec = []
    for h, lg in enumerate(LOG_GAMMA):
        causal = (i >= j) & (i < t_valid) & (j < t_valid)
        dtab[h] = np.where(causal, scale * np.exp(np.maximum(i - j, 0.0) * lg), 0.0)
        qtab[h] = np.where(i < t_valid, np.exp((i + 1.0) * lg), 0.0) * np.ones((1, CHUNK))
        ktab[h] = np.where(jr < t_valid, scale * np.exp(np.maximum(t_valid - 1.0 - jr, 0.0) * lg), 0.0) * np.ones((1, CHUNK))
        sdec.append(float(np.exp(t_valid * lg)))
    return tuple(jnp.asarray(t, F32) for t in (dtab, qtab, ktab, bias)), sdec


class _Chain:
    def __init__(self, proj, state, k_prev, v_prev, u_prev, mixed_ref, ubuf):
        self.proj, self.state, self.k_prev, self.v_prev, self.u_prev = proj, state, k_prev, v_prev, u_prev
        self.mixed_ref, self.ubuf = mixed_ref, ubuf


def _mixer_math(chains, tabs, bias_variant, sdec, conv_w, sinks):
    dtab_ref, qtab_ref, ktab_ref, bias_ref = tabs
    t_rows = chains[0].mixed_ref.shape[0]
    pad = CHUNK - t_rows
    nt = (((1,), (1,)), ((), ()))
    tn = (((0,), (0,)), ((), ()))
    att_base = RET_HEADS * RET_DV
    conv_base = att_base + ATT_Q_HEADS * ATT_HEAD_DIM
    att_scale = ATT_HEAD_DIM ** -0.5
    kv_lanes = [slice(g * ATT_HEAD_DIM, (g + 1) * ATT_HEAD_DIM) for g in range(ATT_KV_HEADS)]
    heads, att_heads = range(RET_HEADS), range(ATT_Q_HEADS)

    def pad_rows(a):
        if pad == 0:
            return a
        return jnp.concatenate([a, jnp.zeros((pad, a.shape[1]), a.dtype)], axis=0)

    for ch in chains:
        proj = ch.proj
        ch.k_cur = pad_rows(proj(OFF_AK, OFF_AK + 128))
        ch.v_cur = pad_rows(proj(OFF_AV, OFF_AV + 128))
        k_all = jnp.concatenate([ch.k_prev, ch.k_cur], axis=0)
        ch.v_all = jnp.concatenate([ch.v_prev, ch.v_cur], axis=0)
        rq = [proj(OFF_RQ + h * RET_DK, OFF_RQ + (h + 1) * RET_DK) for h in heads]
        rk = [pad_rows(proj(OFF_RK + h * RET_DK, OFF_RK + (h + 1) * RET_DK)) for h in heads]
        ch.rv = [pad_rows(proj(OFF_RV + h * RET_DV, OFF_RV + (h + 1) * RET_DV)) for h in heads]
        ch.ret_s = [lax.dot_general(rq[h], rk[h], nt, preferred_element_type=F32) for h in heads]
        ch.inter = [jnp.dot(rq[h], ch.state[h].astype(BF16), preferred_element_type=F32) for h in heads]
        ch.att_s = []
        for hh in att_heads:
            q = proj(OFF_AQ + hh * ATT_HEAD_DIM, OFF_AQ + (hh + 1) * ATT_HEAD_DIM) * att_scale
            ch.att_s.append(lax.dot_general(q, k_all[:, kv_lanes[hh // ATT_GROUP]], nt, preferred_element_type=F32))
        ch.new_state = []
        for h in heads:
            ks = (rk[h].astype(F32) * ktab_ref[h]).astype(BF16)
            kv = lax.dot_general(ks, ch.rv[h], tn, preferred_element_type=F32)
            ch.new_state.append(sdec[h] * ch.state[h] + kv)

    for ch in chains:
        ch.ret_p = [(ch.ret_s[h] * dtab_ref[h]).astype(BF16) for h in heads]
        ch.att_e, ch.att_den = [], []
        for hh in att_heads:
            s = ch.att_s[hh] + bias_ref[bias_variant, hh]
            sink = sinks[hh]
            m = jnp.maximum(jnp.max(s, axis=-1, keepdims=True), sink)
            e = jnp.exp(s - m)
            ch.att_den.append(jnp.sum(e, axis=-1, keepdims=True) + jnp.exp(sink - m))
            ch.att_e.append(e.astype(BF16))
        proj, ubuf = ch.proj, ch.ubuf
        gate_b = proj(OFF_CB, OFF_CB + CONV_CH).astype(F32)
        u = proj(OFF_CC, OFF_CC + CONV_CH).astype(F32) * proj(OFF_CH, OFF_CH + CONV_CH).astype(F32)
        ubuf[6:8, :] = ch.u_prev
        ubuf[8:8 + t_rows, :] = u
        y = conv_w[0:1, :] * ubuf[6:6 + t_rows, :] + conv_w[1:2, :] * ubuf[7:7 + t_rows, :] + conv_w[2:3, :] * u
        ch.mixed_ref[:, conv_base:conv_base + CONV_CH] = (gate_b * y).astype(ch.mixed_ref.dtype)

    for ch in chains:
        ch.intra = [jnp.dot(ch.ret_p[h], ch.rv[h], preferred_element_type=F32) for h in heads]
        ch.att_o = [jnp.dot(ch.att_e[hh], ch.v_all[:, kv_lanes[hh // ATT_GROUP]], preferred_element_type=F32)
                    for hh in att_heads]

    for ch in chains:
        for h in heads:
            gate = ch.proj(OFF_RG + h * RET_DV, OFF_RG + (h + 1) * RET_DV).astype(F32)
            ry = ch.intra[h] + ch.inter[h] * qtab_ref[h]
            ryn = ry * lax.rsqrt(jnp.mean(ry * ry, axis=-1, keepdims=True) + EPS)
            ch.mixed_ref[:, h * RET_DV:(h + 1) * RET_DV] = (_silu(gate) * ryn).astype(ch.mixed_ref.dtype)
        for hh in att_heads:
            c0 = att_base + hh * ATT_HEAD_DIM
            ch.mixed_ref[:, c0:c0 + ATT_HEAD_DIM] = (ch.att_o[hh] / ch.att_den[hh]).astype(ch.mixed_ref.dtype)
    return [(ch.new_state, ch.k_cur, ch.v_cur) for ch in chains]


def _mix_prompt_kernel(sink_ref, *refs, layer, sdec, batch, n_carried):
    proj_refs, refs = refs[:batch], refs[batch:]
    (dtab_ref, qtab_ref, ktab_ref, bias_ref, convw_ref, wout_ref), refs = refs[:6], refs[6 + n_carried:]
    (mixed_ref, sret_ref, knew_ref, vnew_ref, cnew_ref,
     woutb_ref, s_scr, kprev_scr, vprev_scr, uprev_scr, ubuf) = refs
    c = pl.program_id(0)
    woutb_ref[0] = wout_ref[0].astype(BF16)

    @pl.when(c == 0)
    def _():
        s_scr[...] = jnp.zeros_like(s_scr)
        kprev_scr[...] = jnp.zeros_like(kprev_scr)
        vprev_scr[...] = jnp.zeros_like(vprev_scr)
        uprev_scr[...] = jnp.zeros_like(uprev_scr)

    sinks = [sink_ref[layer, hh] for hh in range(ATT_Q_HEADS)]
    bias_variant = jnp.where(c == 0, 0, 1)
    last = c == pl.num_programs(0) - 1
    chains = [_Chain(functools.partial(lambda ref, lo, hi: ref[:, lo:hi], proj_refs[b]),
                     [s_scr[b, h] for h in range(RET_HEADS)], kprev_scr[b], vprev_scr[b], uprev_scr[b],
                     mixed_ref.at[b], ubuf.at[b]) for b in range(batch)]
    results = [_mixer_math([ch], (dtab_ref, qtab_ref, ktab_ref, bias_ref), bias_variant, sdec, convw_ref[0], sinks)[0]
               for ch in chains]
    for b, (new_state, k_cur, v_cur) in enumerate(results):
        for h in range(RET_HEADS):
            s_scr[b, h] = new_state[h]
        kprev_scr[b] = k_cur
        vprev_scr[b] = v_cur
        uprev_scr[b] = ubuf[b, CHUNK + 6:CHUNK + 8, :]

        @pl.when(last)
        def _():
            for h in range(RET_HEADS):
                sret_ref[0, b, h] = new_state[h]
            knew_ref[0, b] = k_cur.astype(F32)
            vnew_ref[0, b] = v_cur.astype(F32)
            cnew_ref[0, b] = ubuf[b, CHUNK + 6:CHUNK + 8, :]


STATE_TAILS = [(RET_HEADS, RET_DK, RET_DV), (WINDOW, 128), (WINDOW, 128), (CONV_WIDTH - 1, CONV_CH)]


def _state_specs(layer, seqs_per_block, seq_block):
    return [pl.BlockSpec((1, seqs_per_block) + tail,
                         functools.partial(lambda z, i: (layer, seq_block(i)) + z, (0,) * len(tail)))
            for tail in STATE_TAILS]


def _state_shapes(n_seq):
    return [jax.ShapeDtypeStruct((DEPTH, n_seq) + tail, F32) for tail in STATE_TAILS]


def _carried_args(carried, n_fixed_inputs):
    if carried is None:
        return [], [], {}
    specs = [pl.BlockSpec(memory_space=pl.ANY)] * len(carried)
    return specs, list(carried), {n_fixed_inputs + k: 1 + k for k in range(len(carried))}


def _mix_prompt_call(layer, proj, batch, seq, conv_w, attn_sinks, tables, sdec, w_out, carried):
    n_chunks = seq // CHUNK
    wrows = D_MODEL // n_chunks
    assert D_MODEL % n_chunks == 0 and wrows % 16 == 0
    const3 = lambda ci: (0, 0, 0)
    const4 = lambda ci: (0, 0, 0, 0)
    proj_specs = [pl.BlockSpec((CHUNK, PROJ_WIDTH), functools.partial(lambda b, ci: (b * n_chunks + ci, 0), b))
                  for b in range(batch)]
    carried_specs, carried_args, aliases = _carried_args(carried, 1 + batch + 6)
    return pl.pallas_call(
        functools.partial(_mix_prompt_kernel, layer=layer, sdec=sdec, batch=batch, n_carried=len(carried_args)),
        grid=(n_chunks,),
        in_specs=[pl.BlockSpec(memory_space=pltpu.SMEM)] + proj_specs + [
            pl.BlockSpec((RET_HEADS, CHUNK, CHUNK), const3),
            pl.BlockSpec((RET_HEADS, CHUNK, CHUNK), const3),
            pl.BlockSpec((RET_HEADS, CHUNK, CHUNK), const3),
            pl.BlockSpec((2, ATT_Q_HEADS, CHUNK, 2 * CHUNK), const4),
            pl.BlockSpec((1, CONV_WIDTH, CONV_CH), lambda ci: (layer, 0, 0)),
            pl.BlockSpec((1, wrows, D_MODEL), lambda ci: (layer, ci, 0)),
        ] + carried_specs,
        out_specs=[pl.BlockSpec((batch, CHUNK, D_MODEL), lambda ci: (0, ci, 0))]
        + _state_specs(layer, batch, lambda ci: 0)
        + [pl.BlockSpec((1, wrows, D_MODEL), lambda ci: (0, ci, 0))],
        out_shape=[jax.ShapeDtypeStruct((batch, seq, D_MODEL), BF16)] + _state_shapes(batch)
        + [jax.ShapeDtypeStruct((1, D_MODEL, D_MODEL), BF16)],
        input_output_aliases=aliases,
        scratch_shapes=[
            pltpu.VMEM((batch, RET_HEADS, RET_DK, RET_DV), F32),
            pltpu.VMEM((batch, CHUNK, 128), BF16),
            pltpu.VMEM((batch, CHUNK, 128), BF16),
            pltpu.VMEM((batch, CONV_WIDTH - 1, CONV_CH), F32),
            pltpu.VMEM((batch, CHUNK + 8, CONV_CH), F32),
        ],
        compiler_params=_params("arbitrary"),
        name="mix_prompt",
    )(attn_sinks, *([proj] * batch), *tables, conv_w, w_out, *carried_args)


def _mix_sample_kernel(sink_ref, proj_ref, dtab_ref, qtab_ref, ktab_ref, bias_ref, convw_ref,
                       sin_ref, kbuf_ref, vbuf_ref, cbuf_ref, *refs, layer, sdec, t_valid, seqs_per_step, n_carried):
    assert 2 * SEQ_PAD == MIX_ROWS and seqs_per_step % 2 == 0
    mixed_ref, sret_ref, knew_ref, vnew_ref, cnew_ref, ubuf, kvbuf, mix_scr = refs[n_carried:]
    sin_ref, kbuf_ref, vbuf_ref, cbuf_ref = sin_ref.at[0], kbuf_ref.at[0], vbuf_ref.at[0], cbuf_ref.at[0]
    sret_ref, knew_ref, vnew_ref, cnew_ref = sret_ref.at[0], knew_ref.at[0], vnew_ref.at[0], cnew_ref.at[0]

    def seq_proj(s, lo, hi):
        pair = proj_ref[(s // 2) * MIX_ROWS:(s // 2 + 1) * MIX_ROWS, lo:hi].astype(F32)
        own = pair[(s % 2) * SEQ_PAD:(s % 2 + 1) * SEQ_PAD]
        return jnp.concatenate([own, jnp.zeros((MIX_ROWS - SEQ_PAD, hi - lo), F32)], axis=0).astype(BF16)

    sinks = [sink_ref[layer, hh] for hh in range(ATT_Q_HEADS)]
    chains = [_Chain(functools.partial(seq_proj, s), [sin_ref[s, h] for h in range(RET_HEADS)],
                     kbuf_ref[s].astype(BF16), vbuf_ref[s].astype(BF16), cbuf_ref[s], mix_scr.at[s], ubuf.at[s])
              for s in range(seqs_per_step)]
    results = _mixer_math(chains, (dtab_ref, qtab_ref, ktab_ref, bias_ref), 0, sdec, convw_ref[0], sinks)
    for s, (new_state, _, _) in enumerate(results):
        for h in range(RET_HEADS):
            sret_ref[s, h] = new_state[h]
        for j, (src_ref, off, dst_ref) in enumerate(((kbuf_ref, OFF_AK, knew_ref), (vbuf_ref, OFF_AV, vnew_ref))):
            kvbuf[s, j, 0:WINDOW, :] = src_ref[s]
            kvbuf[s, j, WINDOW:WINDOW + MIX_ROWS, :] = seq_proj(s, off, off + 128).astype(F32)
            dst_ref[s] = kvbuf[s, j, t_valid:t_valid + WINDOW, :]
        cnew_ref[s] = ubuf[s, 8 + t_valid - 2:8 + t_valid, :]
    for p in range(seqs_per_step // 2):
        pair = jnp.concatenate([mix_scr[2 * p, 0:SEQ_PAD, :], mix_scr[2 * p + 1, 0:SEQ_PAD, :]], axis=0)
        mixed_ref[p * MIX_ROWS:(p + 1) * MIX_ROWS, :] = pair.astype(mixed_ref.dtype)


def _mix_sample_call(layer, proj, row0, conv_w, attn_sinks, tables, sdec, states_in, t_valid, carried):
    nb = states_in[0].shape[1]
    sps = 8
    assert nb % sps == 0 and row0 % (sps * SEQ_PAD) == 0
    blk0 = row0 // (sps * SEQ_PAD)
    const3 = lambda bi: (0, 0, 0)
    const4 = lambda bi: (0, 0, 0, 0)
    carried_specs, carried_args, aliases = _carried_args(carried, 11)
    return pl.pallas_call(
        functools.partial(_mix_sample_kernel, layer=layer, sdec=sdec, t_valid=t_valid, seqs_per_step=sps,
                          n_carried=len(carried_args)),
        grid=(nb // sps,),
        in_specs=[
            pl.BlockSpec(memory_space=pltpu.SMEM),
            pl.BlockSpec((sps * SEQ_PAD, PROJ_WIDTH), lambda bi: (blk0 + bi, 0)),
            pl.BlockSpec((RET_HEADS, MIX_ROWS, CHUNK), const3),
            pl.BlockSpec((RET_HEADS, MIX_ROWS, CHUNK), const3),
            pl.BlockSpec((RET_HEADS, CHUNK, CHUNK), const3),
            pl.BlockSpec((1, ATT_Q_HEADS, MIX_ROWS, 2 * CHUNK), const4),
            pl.BlockSpec((1, CONV_WIDTH, CONV_CH), lambda bi: (layer, 0, 0)),
        ] + _state_specs(layer, sps, lambda bi: bi) + carried_specs,
        out_specs=[pl.BlockSpec((sps * SEQ_PAD, D_MODEL), lambda bi: (bi, 0))]
        + _state_specs(layer, sps, lambda bi: bi),
        out_shape=[jax.ShapeDtypeStruct((nb * SEQ_PAD, D_MODEL), BF16)] + _state_shapes(nb),
        input_output_aliases=aliases,
        scratch_shapes=[
            pltpu.VMEM((sps, MIX_ROWS + 8, CONV_CH), F32),
            pltpu.VMEM((sps, 2, WINDOW + MIX_ROWS, 128), F32),
            pltpu.VMEM((sps, MIX_ROWS, D_MODEL), F32),
        ],
        compiler_params=_params("arbitrary"),
        name="mix_sample",
    )(attn_sinks, proj, *tables, conv_w, *states_in, *carried_args)


def kernel(x_prompt, x_sample, state_ret, cache_win_k, cache_win_v, state_conv, c_prompt, c_sample,
           w_in, w_out, conv_w, attn_sinks, norm_g, w_ada, b_ada, w_ff_gate, w_ff_up, w_ff_down):
    batch, seq, _ = x_prompt.shape
    nb, t_valid, _ = x_sample.shape
    w_buf = cache_win_k.shape[2]
    assert w_buf == WINDOW and seq % CHUNK == 0 and t_valid <= SEQ_PAD and nb + batch <= MOD_ROWS
    rows_p, rows_s = batch * seq, nb * SEQ_PAD
    rows = rows_p + rows_s

    c_all = jnp.concatenate([c_sample, c_prompt, jnp.zeros((MOD_ROWS - nb - batch, D_MODEL), F32)], axis=0)
    mod4 = _ada_call(c_all, w_ada, b_ada).reshape(DEPTH, MOD_ROWS, 1, N_MOD * D_MODEL)
    gains = norm_g.reshape(DEPTH * 4, 1, D_MODEL)

    tiles_out = tiles_down = _RowTiles(256, rows_p, rows_s, seq, nb)
    tm_stream = rows // 6
    assert rows % 6 == 0 and tm_stream % 16 == 0
    tabs_p, sdec_p = _mixer_tables(CHUNK, CHUNK, True)
    tabs_s, sdec_s = _mixer_tables(MIX_ROWS, t_valid, False)

    xp = x_prompt.reshape(rows_p, D_MODEL)
    xs = jnp.pad(x_sample, ((0, 0), (0, SEQ_PAD - t_valid), (0, 0))).reshape(rows_s, D_MODEL)
    states_in = (state_ret, cache_win_k.reshape(DEPTH, nb, w_buf, 128), cache_win_v.reshape(DEPTH, nb, w_buf, 128),
                 state_conv)

    h = _prenorm_call(tiles_out, 0, xp, xs, gains, mod4)
    st_p = st_s = None
    for l in range(DEPTH):
        proj = _inproj_call(l, h, w_in, tm_stream, 1280)
        mixed_p, *st_p, w_out_b = _mix_prompt_call(l, proj, batch, seq, conv_w, attn_sinks, tabs_p, sdec_p, w_out,
                                                   st_p)
        mixed_s, *st_s = _mix_sample_call(l, proj, rows_p, conv_w, attn_sinks, tabs_s, sdec_s, states_in, t_valid,
                                          st_s)
        mixed = (mixed_p.reshape(rows_p, D_MODEL), mixed_s)
        xp, xs, h2 = _resid_call("out_proj", tiles_out, 128, mixed, w_out_b, xp, xs, gains, mod4, l,
                                 1, MOD_G1, (l, 2, MOD_SC2, MOD_SH2))
        hid, w_down_b = _gateup_call(l, h2, w_ff_gate, w_ff_up, w_ff_down, tm_stream, 512)
        next_h = (l + 1, 0, MOD_SC1, MOD_SH1) if l + 1 < DEPTH else None
        res = _resid_call("ffn_down", tiles_down, 128, hid, w_down_b, xp, xs, gains, mod4, l,
                          3, MOD_G2, next_h)
        xp, xs = res[0], res[1]
        if next_h is not None:
            h = res[2]

    kv_shape_p = (DEPTH, batch, w_buf, ATT_KV_HEADS, ATT_HEAD_DIM)
    kv_shape_s = (DEPTH, nb, w_buf, ATT_KV_HEADS, ATT_HEAD_DIM)
    return (
        xp.reshape(batch, seq, D_MODEL),
        xs.reshape(nb, SEQ_PAD, D_MODEL)[:, :t_valid],
        st_p[0], st_s[0],
        st_p[1].reshape(kv_shape_p), st_s[1].reshape(kv_shape_s),
        st_p[2].reshape(kv_shape_p), st_s[2].reshape(kv_shape_s),
        st_p[3], st_s[3],
    )
```

```python
import functools
import math

import numpy as np
import jax
import jax.numpy as jnp
from jax import lax
from jax.experimental import pallas as pl
from jax.experimental.pallas import tpu as pltpu

F32 = jnp.float32
BF16 = jnp.bfloat16

D_MODEL = 2048
DEPTH = 4
RET_DK = 128
RET_DV = 128
RET_HEADS = 8
CHUNK = 128
ATT_HEAD_DIM = 64
ATT_Q_HEADS = 8
ATT_KV_HEADS = 2
ATT_GROUP = 4
WINDOW = 128
CONV_WIDTH = 3
CONV_CH = 512
D_FF = 5632
N_MOD = 6
EPS = 1e-6
NEG_INF = -1e30
PROJ_WIDTH = 6400
OFF_RQ, OFF_RK, OFF_RV, OFF_RG = 0, 1024, 2048, 3072
OFF_AQ, OFF_AK, OFF_AV = 4096, 4608, 4736
OFF_CB, OFF_CC, OFF_CH = 4864, 5376, 5888
MOD_SH1, MOD_SC1, MOD_G1, MOD_SH2, MOD_SC2, MOD_G2 = range(6)

SEQ_PAD = 8
MIX_ROWS = 16
MOD_ROWS = 48
VMEM_LIMIT_V7X = 56 * 1024 * 1024

LOG_GAMMA = [math.log1p(-(2.0 ** (-5.0 - h))) for h in range(RET_HEADS)]
ALIBI_SLOPES = [2.0 ** (-8.0 * (h + 1) / ATT_Q_HEADS) for h in range(ATT_Q_HEADS)]


def _silu(a):
    return a / (1.0 + jnp.exp(-a))


def _rms(x, gain):
    return x * lax.rsqrt(jnp.mean(x * x, axis=-1, keepdims=True) + EPS) * gain


def _norm_mod(x, gain, scale, shift):
    return _rms(x, gain) * (1.0 + scale) + shift


def _params(*sem):
    return pltpu.CompilerParams(dimension_semantics=sem, vmem_limit_bytes=VMEM_LIMIT_V7X)


def _ada_kernel(c_ref, w_ref, b_ref, o_ref):
    a = _silu(c_ref[...]).astype(BF16)
    o_ref[0] = jnp.dot(a, w_ref[0].astype(BF16), preferred_element_type=F32) + b_ref[0]


def _ada_call(c_all, w_ada, b_ada):
    tn = 1024
    n = N_MOD * D_MODEL
    return pl.pallas_call(
        _ada_kernel,
        grid=(DEPTH, n // tn),
        in_specs=[
            pl.BlockSpec((MOD_ROWS, D_MODEL), lambda l, j: (0, 0)),
            pl.BlockSpec((1, D_MODEL, tn), lambda l, j: (l, 0, j)),
            pl.BlockSpec((1, 1, tn), lambda l, j: (l, 0, j)),
        ],
        out_specs=pl.BlockSpec((1, MOD_ROWS, tn), lambda l, j: (l, 0, j)),
        out_shape=jax.ShapeDtypeStruct((DEPTH, MOD_ROWS, n), F32),
        compiler_params=_params("arbitrary", "arbitrary"),
        name="ada_mod",
    )(c_all, w_ada, b_ada.reshape(DEPTH, 1, n))


class _RowTiles:
    def __init__(self, tm, rows_prompt, rows_sample, seq_len, n_seq):
        assert rows_prompt % tm == 0 and rows_sample % tm == 0 and seq_len % tm == 0 and tm % SEQ_PAD == 0
        self.tm = tm
        self.n_prompt = rows_prompt // tm
        self.n_sample = rows_sample // tm
        self.n_tiles = self.n_prompt + self.n_sample
        self.tiles_per_seq = seq_len // tm
        self.seqs_per_tile = tm // SEQ_PAD
        self.prompt_mod_row0 = n_seq

    def all_rows(self, width):
        return pl.BlockSpec((self.tm, width), lambda i: (i, 0))

    def prompt_rows(self, width):
        last = self.n_prompt - 1
        return pl.BlockSpec((self.tm, width), lambda i: (jnp.minimum(i, last), 0))

    def sample_rows(self, width):
        first = self.n_prompt
        return pl.BlockSpec((self.tm, width), lambda i: (jnp.maximum(i - first, 0), 0))

    def prompt_mod(self, layer, chunk):
        last, per, row0 = self.n_prompt - 1, self.tiles_per_seq, self.prompt_mod_row0
        return pl.BlockSpec((1, 1, 1, D_MODEL), lambda i: (layer, row0 + jnp.minimum(i, last) // per, 0, chunk))

    def sample_mod(self, layer, chunk):
        first = self.n_prompt
        return pl.BlockSpec((1, self.seqs_per_tile, 1, D_MODEL),
                            lambda i: (layer, jnp.maximum(i - first, 0), 0, chunk))


def _gain_spec(layer, which):
    return pl.BlockSpec((1, 1, D_MODEL), lambda i: (layer * 4 + which, 0, 0))


def _const_weight_spec(layer, k, n):
    return pl.BlockSpec((1, k, n), lambda i: (layer, 0, 0), pipeline_mode=pl.Buffered(1))


def _for_each_group(n_prompt_tiles, fn):
    i = pl.program_id(0)

    @pl.when(i < n_prompt_tiles)
    def _():
        fn(False)

    @pl.when(i >= n_prompt_tiles)
    def _():
        fn(True)


def _sub_blocks(tm, sub, is_sample):
    for r in range(tm // sub):
        rows = slice(r * sub, (r + 1) * sub)
        if is_sample:
            g = sub // SEQ_PAD
            yield rows, (g, SEQ_PAD, D_MODEL), slice(r * g, (r + 1) * g)
        else:
            yield rows, (1, sub, D_MODEL), slice(0, 1)


def _prenorm_kernel(xp_ref, xs_ref, gain_ref, scp_ref, shp_ref, scs_ref, shs_ref, h_ref, *, tiles, sub):
    def run(is_sample):
        x_ref, sc_ref, sh_ref = (xs_ref, scs_ref, shs_ref) if is_sample else (xp_ref, scp_ref, shp_ref)
        for rows, shape3, seqs in _sub_blocks(tiles.tm, sub, is_sample):
            h = _norm_mod(x_ref[rows, :].reshape(shape3), gain_ref[...], sc_ref[0, seqs], sh_ref[0, seqs])
            h_ref[rows, :] = h.reshape(sub, D_MODEL).astype(BF16)

    _for_each_group(tiles.n_prompt, run)


def _prenorm_call(tiles, layer, xp, xs, gains, mod4):
    return pl.pallas_call(
        functools.partial(_prenorm_kernel, tiles=tiles, sub=128),
        grid=(tiles.n_tiles,),
        in_specs=[
            tiles.prompt_rows(D_MODEL), tiles.sample_rows(D_MODEL), _gain_spec(layer, 0),
            tiles.prompt_mod(layer, MOD_SC1), tiles.prompt_mod(layer, MOD_SH1),
            tiles.sample_mod(layer, MOD_SC1), tiles.sample_mod(layer, MOD_SH1),
        ],
        out_specs=tiles.all_rows(D_MODEL),
        out_shape=jax.ShapeDtypeStruct((tiles.n_tiles * tiles.tm, D_MODEL), BF16),
        compiler_params=_params("arbitrary"),
        name="prenorm",
    )(xp, xs, gains, mod4, mod4, mod4, mod4)


def _inproj_kernel(h_ref, w_ref, o_ref, wb_scr):
    @pl.when(pl.program_id(1) == 0)
    def _():
        wb_scr[...] = w_ref[0].astype(BF16)

    o_ref[...] = jnp.dot(h_ref[...], wb_scr[...], preferred_element_type=F32).astype(o_ref.dtype)


def _inproj_call(layer, h, w_in, tm, tn):
    rows = h.shape[0]
    return pl.pallas_call(
        _inproj_kernel,
        grid=(PROJ_WIDTH // tn, rows // tm),
        in_specs=[
            pl.BlockSpec((tm, D_MODEL), lambda j, i: (i, 0)),
            pl.BlockSpec((1, D_MODEL, tn), lambda j, i: (layer, 0, j)),
        ],
        out_specs=pl.BlockSpec((tm, tn), lambda j, i: (i, j)),
        out_shape=jax.ShapeDtypeStruct((rows, PROJ_WIDTH), BF16),
        scratch_shapes=[pltpu.VMEM((D_MODEL, tn), BF16)],
        compiler_params=_params("arbitrary", "arbitrary"),
        name="in_proj",
    )(h, w_in)


def _gateup_kernel(h_ref, wg_ref, wu_ref, wd_ref, o_ref, wdb_ref, wg_scr, wu_scr):
    @pl.when(pl.program_id(1) == 0)
    def _():
        wg_scr[...] = wg_ref[0].astype(BF16)
        wu_scr[...] = wu_ref[0].astype(BF16)

    wdb_ref[0] = wd_ref[0].astype(BF16)
    h = h_ref[...]
    a = jnp.dot(h, wg_scr[...], preferred_element_type=F32)
    b = jnp.dot(h, wu_scr[...], preferred_element_type=F32)
    o_ref[...] = (_silu(a) * b).astype(o_ref.dtype)


def _gateup_call(layer, h2, w_gate, w_up, w_down, tm, tn):
    rows = h2.shape[0]
    n_i = rows // tm
    n_spread = max(d for d in range(1, n_i + 1) if tn % d == 0 and (tn // d) % 16 == 0)
    wd_rows = tn // n_spread
    wd_block = lambda j, i: j * n_spread + jnp.minimum(i, n_spread - 1)
    w_spec = pl.BlockSpec((1, D_MODEL, tn), lambda j, i: (layer, 0, j))
    return pl.pallas_call(
        _gateup_kernel,
        grid=(D_FF // tn, n_i),
        in_specs=[pl.BlockSpec((tm, D_MODEL), lambda j, i: (i, 0)), w_spec, w_spec,
                  pl.BlockSpec((1, wd_rows, D_MODEL), lambda j, i: (layer, wd_block(j, i), 0))],
        out_specs=[pl.BlockSpec((tm, tn), lambda j, i: (i, j)),
                   pl.BlockSpec((1, wd_rows, D_MODEL), lambda j, i: (0, wd_block(j, i), 0))],
        out_shape=[jax.ShapeDtypeStruct((rows, D_FF), BF16), jax.ShapeDtypeStruct((1, D_FF, D_MODEL), BF16)],
        scratch_shapes=[pltpu.VMEM((D_MODEL, tn), BF16), pltpu.VMEM((D_MODEL, tn), BF16)],
        compiler_params=_params("arbitrary", "arbitrary"),
        name="gate_up",
    )(h2, w_gate, w_up, w_down)


def _resid_kernel(*refs, tiles, sub, emit_h, a_split):
    refs = list(refs)
    ap_ref = refs.pop(0)
    as_ref = refs.pop(0) if a_split else ap_ref
    if emit_h:
        (w_ref, xp_ref, xs_ref, gain1_ref, gp_ref, gs_ref, gain2_ref, scp_ref, shp_ref, scs_ref, shs_ref,
         yp_ref, ys_ref, h_ref) = refs
    else:
        w_ref, xp_ref, xs_ref, gain1_ref, gp_ref, gs_ref, yp_ref, ys_ref = refs

    def run(is_sample):
        a_ref, x_ref, y_ref, g_ref = (as_ref, xs_ref, ys_ref, gs_ref) if is_sample else (ap_ref, xp_ref, yp_ref, gp_ref)
        for rows, shape3, seqs in _sub_blocks(tiles.tm, sub, is_sample):
            o = jnp.dot(a_ref[rows, :], w_ref[0], preferred_element_type=F32).reshape(shape3)
            gated_gain = gain1_ref[...] * g_ref[0, seqs]
            y = x_ref[rows, :].reshape(shape3) + o * lax.rsqrt(jnp.mean(o * o, axis=-1, keepdims=True) + EPS) * gated_gain
            y_ref[rows, :] = y.reshape(sub, D_MODEL)
            if emit_h:
                sc_ref, sh_ref = (scs_ref, shs_ref) if is_sample else (scp_ref, shp_ref)
                scaled_gain = gain2_ref[...] * (1.0 + sc_ref[0, seqs])
                h = y * lax.rsqrt(jnp.mean(y * y, axis=-1, keepdims=True) + EPS) * scaled_gain + sh_ref[0, seqs]
                h_ref[rows, :] = h.reshape(sub, D_MODEL).astype(BF16)

    _for_each_group(tiles.n_prompt, run)


def _resid_call(name, tiles, sub, a, w_b, xp, xs, gains, mod4, layer, gain1, gate_chunk, h_params):
    emit_h = h_params is not None
    a_split = isinstance(a, tuple)
    k_dim = w_b.shape[1]
    if a_split:
        in_specs, args = [tiles.prompt_rows(k_dim), tiles.sample_rows(k_dim)], list(a)
    else:
        in_specs, args = [tiles.all_rows(k_dim)], [a]
    in_specs += [
        _const_weight_spec(0, k_dim, D_MODEL),
        tiles.prompt_rows(D_MODEL), tiles.sample_rows(D_MODEL),
        _gain_spec(layer, gain1), tiles.prompt_mod(layer, gate_chunk), tiles.sample_mod(layer, gate_chunk),
    ]
    args += [w_b, xp, xs, gains, mod4, mod4]
    out_specs = [tiles.prompt_rows(D_MODEL), tiles.sample_rows(D_MODEL)]
    out_shape = [jax.ShapeDtypeStruct(xp.shape, F32), jax.ShapeDtypeStruct(xs.shape, F32)]
    if emit_h:
        hl, hg, hsc, hsh = h_params
        in_specs += [_gain_spec(hl, hg), tiles.prompt_mod(hl, hsc), tiles.prompt_mod(hl, hsh),
                     tiles.sample_mod(hl, hsc), tiles.sample_mod(hl, hsh)]
        args += [gains, mod4, mod4, mod4, mod4]
        out_specs.append(tiles.all_rows(D_MODEL))
        out_shape.append(jax.ShapeDtypeStruct((tiles.n_tiles * tiles.tm, D_MODEL), BF16))
    return pl.pallas_call(
        functools.partial(_resid_kernel, tiles=tiles, sub=sub, emit_h=emit_h, a_split=a_split),
        grid=(tiles.n_tiles,),
        in_specs=in_specs,
        out_specs=out_specs,
        out_shape=out_shape,
        compiler_params=_params("arbitrary"),
        name=name,
    )(*args)


def _mixer_tables(t_rows, t_valid, first_chunk_variant):
    qi = np.arange(t_rows)[:, None]
    kj = np.arange(2 * CHUNK)[None, :]
    dist = qi + CHUNK - kj
    allowed = (dist >= 0) & (dist < WINDOW)
    variants = [allowed & (kj >= CHUNK), allowed] if first_chunk_variant else [allowed]
    bias = np.stack([np.stack([np.where(ok, -slope * dist.astype(np.float64), NEG_INF) for slope in ALIBI_SLOPES])
                     for ok in variants])
    scale = RET_DK ** -0.5
    i = np.arange(t_rows)[:, None].astype(np.float64)
    j = np.arange(CHUNK)[None, :].astype(np.float64)
    jr = np.arange(CHUNK)[:, None].astype(np.float64)
    dtab = np.zeros((RET_HEADS, t_rows, CHUNK), np.float64)
    qtab = np.zeros((RET_HEADS, t_rows, CHUNK), np.float64)
    ktab = np.zeros((RET_HEADS, CHUNK, CHUNK), np.float64)
    sdec = []
    for h, lg in enumerate(LOG_GAMMA):
        causal = (i >= j) & (i < t_valid) & (j < t_valid)
        dtab[h] = np.where(causal, scale * np.exp(np.maximum(i - j, 0.0) * lg), 0.0)
        qtab[h] = np.where(i < t_valid, np.exp((i + 1.0) * lg), 0.0) * np.ones((1, CHUNK))
        ktab[h] = np.where(jr < t_valid, scale * np.exp(np.maximum(t_valid - 1.0 - jr, 0.0) * lg), 0.0) * np.ones((1, CHUNK))
        sdec.append(float(np.exp(t_valid * lg)))
    return tuple(jnp.asarray(t, F32) for t in (dtab, qtab, ktab, bias)), sdec


class _Chain:
    def __init__(self, proj, state, k_prev, v_prev, u_prev, mixed_ref, ubuf):
        self.proj, self.state, self.k_prev, self.v_prev, self.u_prev = proj, state, k_prev, v_prev, u_prev
        self.mixed_ref, self.ubuf = mixed_ref, ubuf


def _mixer_math(chains, tabs, bias_variant, sdec, conv_w, sinks):
    dtab_ref, qtab_ref, ktab_ref, bias_ref = tabs
    t_rows = chains[0].mixed_ref.shape[0]
    pad = CHUNK - t_rows
    nt = (((1,), (1,)), ((), ()))
    tn = (((0,), (0,)), ((), ()))
    att_base = RET_HEADS * RET_DV
    conv_base = att_base + ATT_Q_HEADS * ATT_HEAD_DIM
    att_scale = ATT_HEAD_DIM ** -0.5
    kv_lanes = [slice(g * ATT_HEAD_DIM, (g + 1) * ATT_HEAD_DIM) for g in range(ATT_KV_HEADS)]
    heads, att_heads = range(RET_HEADS), range(ATT_Q_HEADS)

    def pad_rows(a):
        if pad == 0:
            return a
        return jnp.concatenate([a, jnp.zeros((pad, a.shape[1]), a.dtype)], axis=0)

    for ch in chains:
        proj = ch.proj
        ch.k_cur = pad_rows(proj(OFF_AK, OFF_AK + 128))
        ch.v_cur = pad_rows(proj(OFF_AV, OFF_AV + 128))
        k_all = jnp.concatenate([ch.k_prev, ch.k_cur], axis=0)
        ch.v_all = jnp.concatenate([ch.v_prev, ch.v_cur], axis=0)
        rq = [proj(OFF_RQ + h * RET_DK, OFF_RQ + (h + 1) * RET_DK) for h in heads]
        rk = [pad_rows(proj(OFF_RK + h * RET_DK, OFF_RK + (h + 1) * RET_DK)) for h in heads]
        ch.rv = [pad_rows(proj(OFF_RV + h * RET_DV, OFF_RV + (h + 1) * RET_DV)) for h in heads]
        ch.ret_s = [lax.dot_general(rq[h], rk[h], nt, preferred_element_type=F32) for h in heads]
        ch.inter = [jnp.dot(rq[h], ch.state[h].astype(BF16), preferred_element_type=F32) for h in heads]
        ch.att_s = []
        for hh in att_heads:
            q = proj(OFF_AQ + hh * ATT_HEAD_DIM, OFF_AQ + (hh + 1) * ATT_HEAD_DIM) * att_scale
            ch.att_s.append(lax.dot_general(q, k_all[:, kv_lanes[hh // ATT_GROUP]], nt, preferred_element_type=F32))
        ch.new_state = []
        for h in heads:
            ks = (rk[h].astype(F32) * ktab_ref[h]).astype(BF16)
            kv = lax.dot_general(ks, ch.rv[h], tn, preferred_element_type=F32)
            ch.new_state.append(sdec[h] * ch.state[h] + kv)

    for ch in chains:
        ch.ret_p = [(ch.ret_s[h] * dtab_ref[h]).astype(BF16) for h in heads]
        ch.att_e, ch.att_den = [], []
        for hh in att_heads:
            s = ch.att_s[hh] + bias_ref[bias_variant, hh]
            sink = sinks[hh]
            m = jnp.maximum(jnp.max(s, axis=-1, keepdims=True), sink)
            e = jnp.exp(s - m)
            ch.att_den.append(jnp.sum(e, axis=-1, keepdims=True) + jnp.exp(sink - m))
            ch.att_e.append(e.astype(BF16))
        proj, ubuf = ch.proj, ch.ubuf
        gate_b = proj(OFF_CB, OFF_CB + CONV_CH).astype(F32)
        u = proj(OFF_CC, OFF_CC + CONV_CH).astype(F32) * proj(OFF_CH, OFF_CH + CONV_CH).astype(F32)
        ubuf[6:8, :] = ch.u_prev
        ubuf[8:8 + t_rows, :] = u
        y = conv_w[0:1, :] * ubuf[6:6 + t_rows, :] + conv_w[1:2, :] * ubuf[7:7 + t_rows, :] + conv_w[2:3, :] * u
        ch.mixed_ref[:, conv_base:conv_base + CONV_CH] = (gate_b * y).astype(ch.mixed_ref.dtype)

    for ch in chains:
        ch.intra = [jnp.dot(ch.ret_p[h], ch.rv[h], preferred_element_type=F32) for h in heads]
        ch.att_o = [jnp.dot(ch.att_e[hh], ch.v_all[:, kv_lanes[hh // ATT_GROUP]], preferred_element_type=F32)
                    for hh in att_heads]

    for ch in chains:
        for h in heads:
            gate = ch.proj(OFF_RG + h * RET_DV, OFF_RG + (h + 1) * RET_DV).astype(F32)
            ry = ch.intra[h] + ch.inter[h] * qtab_ref[h]
            ryn = ry * lax.rsqrt(jnp.mean(ry * ry, axis=-1, keepdims=True) + EPS)
            ch.mixed_ref[:, h * RET_DV:(h + 1) * RET_DV] = (_silu(gate) * ryn).astype(ch.mixed_ref.dtype)
        for hh in att_heads:
            c0 = att_base + hh * ATT_HEAD_DIM
            ch.mixed_ref[:, c0:c0 + ATT_HEAD_DIM] = (ch.att_o[hh] / ch.att_den[hh]).astype(ch.mixed_ref.dtype)
    return [(ch.new_state, ch.k_cur, ch.v_cur) for ch in chains]


def _mix_prompt_kernel(sink_ref, *refs, layer, sdec, batch, n_carried):
    proj_refs, refs = refs[:batch], refs[batch:]
    (dtab_ref, qtab_ref, ktab_ref, bias_ref, convw_ref, wout_ref), refs = refs[:6], refs[6 + n_carried:]
    (mixed_ref, sret_ref, knew_ref, vnew_ref, cnew_ref,
     woutb_ref, s_scr, kprev_scr, vprev_scr, uprev_scr, ubuf) = refs
    c = pl.program_id(0)
    woutb_ref[0] = wout_ref[0].astype(BF16)

    @pl.when(c == 0)
    def _():
        s_scr[...] = jnp.zeros_like(s_scr)
        kprev_scr[...] = jnp.zeros_like(kprev_scr)
        vprev_scr[...] = jnp.zeros_like(vprev_scr)
        uprev_scr[...] = jnp.zeros_like(uprev_scr)

    sinks = [sink_ref[layer, hh] for hh in range(ATT_Q_HEADS)]
    bias_variant = jnp.where(c == 0, 0, 1)
    last = c == pl.num_programs(0) - 1
    chains = [_Chain(functools.partial(lambda ref, lo, hi: ref[:, lo:hi], proj_refs[b]),
                     [s_scr[b, h] for h in range(RET_HEADS)], kprev_scr[b], vprev_scr[b], uprev_scr[b],
                     mixed_ref.at[b], ubuf.at[b]) for b in range(batch)]
    results = [_mixer_math([ch], (dtab_ref, qtab_ref, ktab_ref, bias_ref), bias_variant, sdec, convw_ref[0], sinks)[0]
               for ch in chains]
    for b, (new_state, k_cur, v_cur) in enumerate(results):
        for h in range(RET_HEADS):
            s_scr[b, h] = new_state[h]
        kprev_scr[b] = k_cur
        vprev_scr[b] = v_cur
        uprev_scr[b] = ubuf[b, CHUNK + 6:CHUNK + 8, :]

        @pl.when(last)
        def _():
            for h in range(RET_HEADS):
                sret_ref[0, b, h] = new_state[h]
            knew_ref[0, b] = k_cur.astype(F32)
            vnew_ref[0, b] = v_cur.astype(F32)
            cnew_ref[0, b] = ubuf[b, CHUNK + 6:CHUNK + 8, :]


STATE_TAILS = [(RET_HEADS, RET_DK, RET_DV), (WINDOW, 128), (WINDOW, 128), (CONV_WIDTH - 1, CONV_CH)]


def _state_specs(layer, seqs_per_block, seq_block):
    return [pl.BlockSpec((1, seqs_per_block) + tail,
                         functools.partial(lambda z, i: (layer, seq_block(i)) + z, (0,) * len(tail)))
            for tail in STATE_TAILS]


def _state_shapes(n_seq):
    return [jax.ShapeDtypeStruct((DEPTH, n_seq) + tail, F32) for tail in STATE_TAILS]


def _carried_args(carried, n_fixed_inputs):
    if carried is None:
        return [], [], {}
    specs = [pl.BlockSpec(memory_space=pl.ANY)] * len(carried)
    return specs, list(carried), {n_fixed_inputs + k: 1 + k for k in range(len(carried))}


def _mix_prompt_call(layer, proj, batch, seq, conv_w, attn_sinks, tables, sdec, w_out, carried):
    n_chunks = seq // CHUNK
    wrows = D_MODEL // n_chunks
    assert D_MODEL % n_chunks == 0 and wrows % 16 == 0
    const3 = lambda ci: (0, 0, 0)
    const4 = lambda ci: (0, 0, 0, 0)
    proj_specs = [pl.BlockSpec((CHUNK, PROJ_WIDTH), functools.partial(lambda b, ci: (b * n_chunks + ci, 0), b))
                  for b in range(batch)]
    carried_specs, carried_args, aliases = _carried_args(carried, 1 + batch + 6)
    return pl.pallas_call(
        functools.partial(_mix_prompt_kernel, layer=layer, sdec=sdec, batch=batch, n_carried=len(carried_args)),
        grid=(n_chunks,),
        in_specs=[pl.BlockSpec(memory_space=pltpu.SMEM)] + proj_specs + [
            pl.BlockSpec((RET_HEADS, CHUNK, CHUNK), const3),
            pl.BlockSpec((RET_HEADS, CHUNK, CHUNK), const3),
            pl.BlockSpec((RET_HEADS, CHUNK, CHUNK), const3),
            pl.BlockSpec((2, ATT_Q_HEADS, CHUNK, 2 * CHUNK), const4),
            pl.BlockSpec((1, CONV_WIDTH, CONV_CH), lambda ci: (layer, 0, 0)),
            pl.BlockSpec((1, wrows, D_MODEL), lambda ci: (layer, ci, 0)),
        ] + carried_specs,
        out_specs=[pl.BlockSpec((batch, CHUNK, D_MODEL), lambda ci: (0, ci, 0))]
        + _state_specs(layer, batch, lambda ci: 0)
        + [pl.BlockSpec((1, wrows, D_MODEL), lambda ci: (0, ci, 0))],
        out_shape=[jax.ShapeDtypeStruct((batch, seq, D_MODEL), BF16)] + _state_shapes(batch)
        + [jax.ShapeDtypeStruct((1, D_MODEL, D_MODEL), BF16)],
        input_output_aliases=aliases,
        scratch_shapes=[
            pltpu.VMEM((batch, RET_HEADS, RET_DK, RET_DV), F32),
            pltpu.VMEM((batch, CHUNK, 128), BF16),
            pltpu.VMEM((batch, CHUNK, 128), BF16),
            pltpu.VMEM((batch, CONV_WIDTH - 1, CONV_CH), F32),
            pltpu.VMEM((batch, CHUNK + 8, CONV_CH), F32),
        ],
        compiler_params=_params("arbitrary"),
        name="mix_prompt",
    )(attn_sinks, *([proj] * batch), *tables, conv_w, w_out, *carried_args)


def _mix_sample_kernel(sink_ref, proj_ref, dtab_ref, qtab_ref, ktab_ref, bias_ref, convw_ref,
                       sin_ref, kbuf_ref, vbuf_ref, cbuf_ref, *refs, layer, sdec, t_valid, seqs_per_step, n_carried):
    assert 2 * SEQ_PAD == MIX_ROWS and seqs_per_step % 2 == 0
    mixed_ref, sret_ref, knew_ref, vnew_ref, cnew_ref, ubuf, kvbuf, mix_scr = refs[n_carried:]
    sin_ref, kbuf_ref, vbuf_ref, cbuf_ref = sin_ref.at[0], kbuf_ref.at[0], vbuf_ref.at[0], cbuf_ref.at[0]
    sret_ref, knew_ref, vnew_ref, cnew_ref = sret_ref.at[0], knew_ref.at[0], vnew_ref.at[0], cnew_ref.at[0]

    def seq_proj(s, lo, hi):
        pair = proj_ref[(s // 2) * MIX_ROWS:(s // 2 + 1) * MIX_ROWS, lo:hi].astype(F32)
        own = pair[(s % 2) * SEQ_PAD:(s % 2 + 1) * SEQ_PAD]
        return jnp.concatenate([own, jnp.zeros((MIX_ROWS - SEQ_PAD, hi - lo), F32)], axis=0).astype(BF16)

    sinks = [sink_ref[layer, hh] for hh in range(ATT_Q_HEADS)]
    chains = [_Chain(functools.partial(seq_proj, s), [sin_ref[s, h] for h in range(RET_HEADS)],
                     kbuf_ref[s].astype(BF16), vbuf_ref[s].astype(BF16), cbuf_ref[s], mix_scr.at[s], ubuf.at[s])
              for s in range(seqs_per_step)]
    results = _mixer_math(chains, (dtab_ref, qtab_ref, ktab_ref, bias_ref), 0, sdec, convw_ref[0], sinks)
    for s, (new_state, _, _) in enumerate(results):
        for h in range(RET_HEADS):
            sret_ref[s, h] = new_state[h]
        for j, (src_ref, off, dst_ref) in enumerate(((kbuf_ref, OFF_AK, knew_ref), (vbuf_ref, OFF_AV, vnew_ref))):
            kvbuf[s, j, 0:WINDOW, :] = src_ref[s]
            kvbuf[s, j, WINDOW:WINDOW + MIX_ROWS, :] = seq_proj(s, off, off + 128).astype(F32)
            dst_ref[s] = kvbuf[s, j, t_valid:t_valid + WINDOW, :]
        cnew_ref[s] = ubuf[s, 8 + t_valid - 2:8 + t_valid, :]
    for p in range(seqs_per_step // 2):
        pair = jnp.concatenate([mix_scr[2 * p, 0:SEQ_PAD, :], mix_scr[2 * p + 1, 0:SEQ_PAD, :]], axis=0)
        mixed_ref[p * MIX_ROWS:(p + 1) * MIX_ROWS, :] = pair.astype(mixed_ref.dtype)


def _mix_sample_call(layer, proj, row0, conv_w, attn_sinks, tables, sdec, states_in, t_valid, carried):
    nb = states_in[0].shape[1]
    sps = 8
    assert nb % sps == 0 and row0 % (sps * SEQ_PAD) == 0
    blk0 = row0 // (sps * SEQ_PAD)
    const3 = lambda bi: (0, 0, 0)
    const4 = lambda bi: (0, 0, 0, 0)
    carried_specs, carried_args, aliases = _carried_args(carried, 11)
    return pl.pallas_call(
        functools.partial(_mix_sample_kernel, layer=layer, sdec=sdec, t_valid=t_valid, seqs_per_step=sps,
                          n_carried=len(carried_args)),
        grid=(nb // sps,),
        in_specs=[
            pl.BlockSpec(memory_space=pltpu.SMEM),
            pl.BlockSpec((sps * SEQ_PAD, PROJ_WIDTH), lambda bi: (blk0 + bi, 0)),
            pl.BlockSpec((RET_HEADS, MIX_ROWS, CHUNK), const3),
            pl.BlockSpec((RET_HEADS, MIX_ROWS, CHUNK), const3),
            pl.BlockSpec((RET_HEADS, CHUNK, CHUNK), const3),
            pl.BlockSpec((1, ATT_Q_HEADS, MIX_ROWS, 2 * CHUNK), const4),
            pl.BlockSpec((1, CONV_WIDTH, CONV_CH), lambda bi: (layer, 0, 0)),
        ] + _state_specs(layer, sps, lambda bi: bi) + carried_specs,
        out_specs=[pl.BlockSpec((sps * SEQ_PAD, D_MODEL), lambda bi: (bi, 0))]
        + _state_specs(layer, sps, lambda bi: bi),
        out_shape=[jax.ShapeDtypeStruct((nb * SEQ_PAD, D_MODEL), BF16)] + _state_shapes(nb),
        input_output_aliases=aliases,
        scratch_shapes=[
            pltpu.VMEM((sps, MIX_ROWS + 8, CONV_CH), F32),
            pltpu.VMEM((sps, 2, WINDOW + MIX_ROWS, 128), F32),
            pltpu.VMEM((sps, MIX_ROWS, D_MODEL), F32),
        ],
        compiler_params=_params("arbitrary"),
        name="mix_sample",
    )(attn_sinks, proj, *tables, conv_w, *states_in, *carried_args)


def kernel(x_prompt, x_sample, state_ret, cache_win_k, cache_win_v, state_conv, c_prompt, c_sample,
           w_in, w_out, conv_w, attn_sinks, norm_g, w_ada, b_ada, w_ff_gate, w_ff_up, w_ff_down):
    batch, seq, _ = x_prompt.shape
    nb, t_valid, _ = x_sample.shape
    w_buf = cache_win_k.shape[2]
    assert w_buf == WINDOW and seq % CHUNK == 0 and t_valid <= SEQ_PAD and nb + batch <= MOD_ROWS
    rows_p, rows_s = batch * seq, nb * SEQ_PAD
    rows = rows_p + rows_s

    c_all = jnp.concatenate([c_sample, c_prompt, jnp.zeros((MOD_ROWS - nb - batch, D_MODEL), F32)], axis=0)
    mod4 = _ada_call(c_all, w_ada, b_ada).reshape(DEPTH, MOD_ROWS, 1, N_MOD * D_MODEL)
    gains = norm_g.reshape(DEPTH * 4, 1, D_MODEL)

    tiles_out = tiles_down = _RowTiles(256, rows_p, rows_s, seq, nb)
    tm_inproj, tm_gateup = rows // 6, rows // 8
    assert rows % 24 == 0 and tm_inproj % 16 == 0 and tm_gateup % 16 == 0
    tabs_p, sdec_p = _mixer_tables(CHUNK, CHUNK, True)
    tabs_s, sdec_s = _mixer_tables(MIX_ROWS, t_valid, False)

    xp = x_prompt.reshape(rows_p, D_MODEL)
    xs = jnp.pad(x_sample, ((0, 0), (0, SEQ_PAD - t_valid), (0, 0))).reshape(rows_s, D_MODEL)
    states_in = (state_ret, cache_win_k.reshape(DEPTH, nb, w_buf, 128), cache_win_v.reshape(DEPTH, nb, w_buf, 128),
                 state_conv)

    h = _prenorm_call(tiles_out, 0, xp, xs, gains, mod4)
    st_p = st_s = None
    for l in range(DEPTH):
        proj = _inproj_call(l, h, w_in, tm_inproj, 1280)
        mixed_p, *st_p, w_out_b = _mix_prompt_call(l, proj, batch, seq, conv_w, attn_sinks, tabs_p, sdec_p, w_out,
                                                   st_p)
        mixed_s, *st_s = _mix_sample_call(l, proj, rows_p, conv_w, attn_sinks, tabs_s, sdec_s, states_in, t_valid,
                                          st_s)
        mixed = (mixed_p.reshape(rows_p, D_MODEL), mixed_s)
        xp, xs, h2 = _resid_call("out_proj", tiles_out, 128, mixed, w_out_b, xp, xs, gains, mod4, l,
                                 1, MOD_G1, (l, 2, MOD_SC2, MOD_SH2))
        hid, w_down_b = _gateup_call(l, h2, w_ff_gate, w_ff_up, w_ff_down, tm_gateup, 512)
        next_h = (l + 1, 0, MOD_SC1, MOD_SH1) if l + 1 < DEPTH else None
        res = _resid_call("ffn_down", tiles_down, 128, hid, w_down_b, xp, xs, gains, mod4, l,
                          3, MOD_G2, next_h)
        xp, xs = res[0], res[1]
        if next_h is not None:
            h = res[2]

    kv_shape_p = (DEPTH, batch, w_buf, ATT_KV_HEADS, ATT_HEAD_DIM)
    kv_shape_s = (DEPTH, nb, w_buf, ATT_KV_HEADS, ATT_HEAD_DIM)
    return (
        xp.reshape(batch, seq, D_MODEL),
        xs.reshape(nb, SEQ_PAD, D_MODEL)[:, :t_valid],
        st_p[0], st_s[0],
        st_p[1].reshape(kv_shape_p), st_s[1].reshape(kv_shape_s),
        st_p[2].reshape(kv_shape_p), st_s[2].reshape(kv_shape_s),
        st_p[3], st_s[3],
    )
```

```python
import functools
import math

import numpy as np
import jax
import jax.numpy as jnp
from jax import lax
from jax.experimental import pallas as pl
from jax.experimental.pallas import tpu as pltpu

F32 = jnp.float32
BF16 = jnp.bfloat16

D_MODEL = 2048
DEPTH = 4
RET_DK = 128
RET_DV = 128
RET_HEADS = 8
CHUNK = 128
ATT_HEAD_DIM = 64
ATT_Q_HEADS = 8
ATT_KV_HEADS = 2
ATT_GROUP = 4
WINDOW = 128
CONV_WIDTH = 3
CONV_CH = 512
D_FF = 5632
N_MOD = 6
EPS = 1e-6
NEG_INF = -1e30
PROJ_WIDTH = 6400
OFF_RQ, OFF_RK, OFF_RV, OFF_RG = 0, 1024, 2048, 3072
OFF_AQ, OFF_AK, OFF_AV = 4096, 4608, 4736
OFF_CB, OFF_CC, OFF_CH = 4864, 5376, 5888
MOD_SH1, MOD_SC1, MOD_G1, MOD_SH2, MOD_SC2, MOD_G2 = range(6)

SEQ_PAD = 8
MIX_ROWS = 16
MOD_ROWS = 48
VMEM_LIMIT_V7X = 56 * 1024 * 1024

LOG_GAMMA = [math.log1p(-(2.0 ** (-5.0 - h))) for h in range(RET_HEADS)]
ALIBI_SLOPES = [2.0 ** (-8.0 * (h + 1) / ATT_Q_HEADS) for h in range(ATT_Q_HEADS)]


def _silu(a):
    return a / (1.0 + jnp.exp(-a))


def _rms(x, gain):
    return x * lax.rsqrt(jnp.mean(x * x, axis=-1, keepdims=True) + EPS) * gain


def _norm_mod(x, gain, scale, shift):
    return _rms(x, gain) * (1.0 + scale) + shift


def _params(*sem):
    return pltpu.CompilerParams(dimension_semantics=sem, vmem_limit_bytes=VMEM_LIMIT_V7X)


def _ada_kernel(c_ref, w_ref, b_ref, o_ref):
    a = _silu(c_ref[...]).astype(BF16)
    o_ref[0] = jnp.dot(a, w_ref[0].astype(BF16), preferred_element_type=F32) + b_ref[0]


def _ada_call(c_all, w_ada, b_ada):
    tn = 2048
    n = N_MOD * D_MODEL
    return pl.pallas_call(
        _ada_kernel,
        grid=(DEPTH, n // tn),
        in_specs=[
            pl.BlockSpec((MOD_ROWS, D_MODEL), lambda l, j: (0, 0)),
            pl.BlockSpec((1, D_MODEL, tn), lambda l, j: (l, 0, j)),
            pl.BlockSpec((1, 1, tn), lambda l, j: (l, 0, j)),
        ],
        out_specs=pl.BlockSpec((1, MOD_ROWS, tn), lambda l, j: (l, 0, j)),
        out_shape=jax.ShapeDtypeStruct((DEPTH, MOD_ROWS, n), F32),
        compiler_params=_params("arbitrary", "arbitrary"),
        name="ada_mod",
    )(c_all, w_ada, b_ada.reshape(DEPTH, 1, n))


class _RowTiles:
    def __init__(self, tm, rows_prompt, rows_sample, seq_len, n_seq):
        assert rows_prompt % tm == 0 and rows_sample % tm == 0 and seq_len % tm == 0 and tm % SEQ_PAD == 0
        self.tm = tm
        self.n_prompt = rows_prompt // tm
        self.n_sample = rows_sample // tm
        self.n_tiles = self.n_prompt + self.n_sample
        self.tiles_per_seq = seq_len // tm
        self.seqs_per_tile = tm // SEQ_PAD
        self.prompt_mod_row0 = n_seq

    def all_rows(self, width):
        return pl.BlockSpec((self.tm, width), lambda i: (i, 0))

    def prompt_rows(self, width):
        last = self.n_prompt - 1
        return pl.BlockSpec((self.tm, width), lambda i: (jnp.minimum(i, last), 0))

    def sample_rows(self, width):
        first = self.n_prompt
        return pl.BlockSpec((self.tm, width), lambda i: (jnp.maximum(i - first, 0), 0))

    def prompt_mod(self, layer, chunk):
        last, per, row0 = self.n_prompt - 1, self.tiles_per_seq, self.prompt_mod_row0
        return pl.BlockSpec((1, 1, 1, D_MODEL), lambda i: (layer, row0 + jnp.minimum(i, last) // per, 0, chunk))

    def sample_mod(self, layer, chunk):
        first = self.n_prompt
        return pl.BlockSpec((1, self.seqs_per_tile, 1, D_MODEL),
                            lambda i: (layer, jnp.maximum(i - first, 0), 0, chunk))


def _gain_spec(layer, which):
    return pl.BlockSpec((1, 1, D_MODEL), lambda i: (layer * 4 + which, 0, 0))


def _const_weight_spec(layer, k, n):
    return pl.BlockSpec((1, k, n), lambda i: (layer, 0, 0), pipeline_mode=pl.Buffered(1))


def _for_each_group(n_prompt_tiles, fn):
    i = pl.program_id(0)

    @pl.when(i < n_prompt_tiles)
    def _():
        fn(False)

    @pl.when(i >= n_prompt_tiles)
    def _():
        fn(True)


def _sub_blocks(tm, sub, is_sample):
    for r in range(tm // sub):
        rows = slice(r * sub, (r + 1) * sub)
        if is_sample:
            g = sub // SEQ_PAD
            yield rows, (g, SEQ_PAD, D_MODEL), slice(r * g, (r + 1) * g)
        else:
            yield rows, (1, sub, D_MODEL), slice(0, 1)


def _prenorm_kernel(xp_ref, xs_ref, gain_ref, scp_ref, shp_ref, scs_ref, shs_ref, h_ref, *, tiles, sub):
    def run(is_sample):
        x_ref, sc_ref, sh_ref = (xs_ref, scs_ref, shs_ref) if is_sample else (xp_ref, scp_ref, shp_ref)
        for rows, shape3, seqs in _sub_blocks(tiles.tm, sub, is_sample):
            h = _norm_mod(x_ref[rows, :].reshape(shape3), gain_ref[...], sc_ref[0, seqs], sh_ref[0, seqs])
            h_ref[rows, :] = h.reshape(sub, D_MODEL).astype(BF16)

    _for_each_group(tiles.n_prompt, run)


def _prenorm_call(tiles, layer, xp, xs, gains, mod4):
    return pl.pallas_call(
        functools.partial(_prenorm_kernel, tiles=tiles, sub=256),
        grid=(tiles.n_tiles,),
        in_specs=[
            tiles.prompt_rows(D_MODEL), tiles.sample_rows(D_MODEL), _gain_spec(layer, 0),
            tiles.prompt_mod(layer, MOD_SC1), tiles.prompt_mod(layer, MOD_SH1),
            tiles.sample_mod(layer, MOD_SC1), tiles.sample_mod(layer, MOD_SH1),
        ],
        out_specs=tiles.all_rows(D_MODEL),
        out_shape=jax.ShapeDtypeStruct((tiles.n_tiles * tiles.tm, D_MODEL), BF16),
        compiler_params=_params("arbitrary"),
        name="prenorm",
    )(xp, xs, gains, mod4, mod4, mod4, mod4)


def _inproj_kernel(h_ref, w_ref, o_ref, wb_scr):
    @pl.when(pl.program_id(1) == 0)
    def _():
        wb_scr[...] = w_ref[0].astype(BF16)

    o_ref[...] = jnp.dot(h_ref[...], wb_scr[...], preferred_element_type=F32).astype(o_ref.dtype)


def _inproj_call(layer, h, w_in, tm, tn):
    rows = h.shape[0]
    return pl.pallas_call(
        _inproj_kernel,
        grid=(PROJ_WIDTH // tn, rows // tm),
        in_specs=[
            pl.BlockSpec((tm, D_MODEL), lambda j, i: (i, 0)),
            pl.BlockSpec((1, D_MODEL, tn), lambda j, i: (layer, 0, j)),
        ],
        out_specs=pl.BlockSpec((tm, tn), lambda j, i: (i, j)),
        out_shape=jax.ShapeDtypeStruct((rows, PROJ_WIDTH), BF16),
        scratch_shapes=[pltpu.VMEM((D_MODEL, tn), BF16)],
        compiler_params=_params("arbitrary", "arbitrary"),
        name="in_proj",
    )(h, w_in)


def _gateup_kernel(h_ref, wg_ref, wu_ref, wd_ref, o_ref, wdb_ref, wg_scr, wu_scr):
    @pl.when(pl.program_id(1) == 0)
    def _():
        wg_scr[...] = wg_ref[0].astype(BF16)
        wu_scr[...] = wu_ref[0].astype(BF16)

    wdb_ref[0] = wd_ref[0].astype(BF16)
    h = h_ref[...]
    a = jnp.dot(h, wg_scr[...], preferred_element_type=F32)
    b = jnp.dot(h, wu_scr[...], preferred_element_type=F32)
    o_ref[...] = (_silu(a) * b).astype(o_ref.dtype)


def _gateup_call(layer, h2, w_gate, w_up, w_down, tm, tn):
    rows = h2.shape[0]
    n_i = rows // tm
    n_spread = max(d for d in range(1, n_i + 1) if tn % d == 0 and (tn // d) % 16 == 0)
    wd_rows = tn // n_spread
    wd_block = lambda j, i: j * n_spread + jnp.minimum(i, n_spread - 1)
    w_spec = pl.BlockSpec((1, D_MODEL, tn), lambda j, i: (layer, 0, j))
    return pl.pallas_call(
        _gateup_kernel,
        grid=(D_FF // tn, n_i),
        in_specs=[pl.BlockSpec((tm, D_MODEL), lambda j, i: (i, 0)), w_spec, w_spec,
                  pl.BlockSpec((1, wd_rows, D_MODEL), lambda j, i: (layer, wd_block(j, i), 0))],
        out_specs=[pl.BlockSpec((tm, tn), lambda j, i: (i, j)),
                   pl.BlockSpec((1, wd_rows, D_MODEL), lambda j, i: (0, wd_block(j, i), 0))],
        out_shape=[jax.ShapeDtypeStruct((rows, D_FF), BF16), jax.ShapeDtypeStruct((1, D_FF, D_MODEL), BF16)],
        scratch_shapes=[pltpu.VMEM((D_MODEL, tn), BF16), pltpu.VMEM((D_MODEL, tn), BF16)],
        compiler_params=_params("arbitrary", "arbitrary"),
        name="gate_up",
    )(h2, w_gate, w_up, w_down)


def _resid_kernel(*refs, tiles, sub, emit_h, a_split):
    refs = list(refs)
    ap_ref = refs.pop(0)
    as_ref = refs.pop(0) if a_split else ap_ref
    if emit_h:
        (w_ref, xp_ref, xs_ref, gain1_ref, gp_ref, gs_ref, gain2_ref, scp_ref, shp_ref, scs_ref, shs_ref,
         yp_ref, ys_ref, h_ref) = refs
    else:
        w_ref, xp_ref, xs_ref, gain1_ref, gp_ref, gs_ref, yp_ref, ys_ref = refs

    def run(is_sample):
        a_ref, x_ref, y_ref, g_ref = (as_ref, xs_ref, ys_ref, gs_ref) if is_sample else (ap_ref, xp_ref, yp_ref, gp_ref)
        for rows, shape3, seqs in _sub_blocks(tiles.tm, sub, is_sample):
            o = jnp.dot(a_ref[rows, :], w_ref[0], preferred_element_type=F32).reshape(shape3)
            gated_gain = gain1_ref[...] * g_ref[0, seqs]
            y = x_ref[rows, :].reshape(shape3) + o * lax.rsqrt(jnp.mean(o * o, axis=-1, keepdims=True) + EPS) * gated_gain
            y_ref[rows, :] = y.reshape(sub, D_MODEL)
            if emit_h:
                sc_ref, sh_ref = (scs_ref, shs_ref) if is_sample else (scp_ref, shp_ref)
                scaled_gain = gain2_ref[...] * (1.0 + sc_ref[0, seqs])
                h = y * lax.rsqrt(jnp.mean(y * y, axis=-1, keepdims=True) + EPS) * scaled_gain + sh_ref[0, seqs]
                h_ref[rows, :] = h.reshape(sub, D_MODEL).astype(BF16)

    _for_each_group(tiles.n_prompt, run)


def _resid_call(name, tiles, sub, a, w_b, xp, xs, gains, mod4, layer, gain1, gate_chunk, h_params):
    emit_h = h_params is not None
    a_split = isinstance(a, tuple)
    k_dim = w_b.shape[1]
    if a_split:
        in_specs, args = [tiles.prompt_rows(k_dim), tiles.sample_rows(k_dim)], list(a)
    else:
        in_specs, args = [tiles.all_rows(k_dim)], [a]
    in_specs += [
        _const_weight_spec(0, k_dim, D_MODEL),
        tiles.prompt_rows(D_MODEL), tiles.sample_rows(D_MODEL),
        _gain_spec(layer, gain1), tiles.prompt_mod(layer, gate_chunk), tiles.sample_mod(layer, gate_chunk),
    ]
    args += [w_b, xp, xs, gains, mod4, mod4]
    out_specs = [tiles.prompt_rows(D_MODEL), tiles.sample_rows(D_MODEL)]
    out_shape = [jax.ShapeDtypeStruct(xp.shape, F32), jax.ShapeDtypeStruct(xs.shape, F32)]
    if emit_h:
        hl, hg, hsc, hsh = h_params
        in_specs += [_gain_spec(hl, hg), tiles.prompt_mod(hl, hsc), tiles.prompt_mod(hl, hsh),
                     tiles.sample_mod(hl, hsc), tiles.sample_mod(hl, hsh)]
        args += [gains, mod4, mod4, mod4, mod4]
        out_specs.append(tiles.all_rows(D_MODEL))
        out_shape.append(jax.ShapeDtypeStruct((tiles.n_tiles * tiles.tm, D_MODEL), BF16))
    return pl.pallas_call(
        functools.partial(_resid_kernel, tiles=tiles, sub=sub, emit_h=emit_h, a_split=a_split),
        grid=(tiles.n_tiles,),
        in_specs=in_specs,
        out_specs=out_specs,
        out_shape=out_shape,
        compiler_params=_params("arbitrary"),
        name=name,
    )(*args)


def _mixer_tables(t_rows, t_valid, first_chunk_variant):
    qi = np.arange(t_rows)[:, None]
    kj = np.arange(2 * CHUNK)[None, :]
    dist = qi + CHUNK - kj
    allowed = (dist >= 0) & (dist < WINDOW)
    variants = [allowed & (kj >= CHUNK), allowed] if first_chunk_variant else [allowed]
    bias = np.stack([np.stack([np.where(ok, -slope * dist.astype(np.float64), NEG_INF) for slope in ALIBI_SLOPES])
                     for ok in variants])
    scale = RET_DK ** -0.5
    i = np.arange(t_rows)[:, None].astype(np.float64)
    j = np.arange(CHUNK)[None, :].astype(np.float64)
    jr = np.arange(CHUNK)[:, None].astype(np.float64)
    dtab = np.zeros((RET_HEADS, t_rows, CHUNK), np.float64)
    qtab = np.zeros((RET_HEADS, t_rows, CHUNK), np.float64)
    ktab = np.zeros((RET_HEADS, CHUNK, CHUNK), np.float64)
    sdec = []
    for h, lg in enumerate(LOG_GAMMA):
        causal = (i >= j) & (i < t_valid) & (j < t_valid)
        dtab[h] = np.where(causal, scale * np.exp(np.maximum(i - j, 0.0) * lg), 0.0)
        qtab[h] = np.where(i < t_valid, np.exp((i + 1.0) * lg), 0.0) * np.ones((1, CHUNK))
        ktab[h] = np.where(jr < t_valid, scale * np.exp(np.maximum(t_valid - 1.0 - jr, 0.0) * lg), 0.0) * np.ones((1, CHUNK))
        sdec.append(float(np.exp(t_valid * lg)))
    return tuple(jnp.asarray(t, F32) for t in (dtab, qtab, ktab, bias)), sdec


class _Chain:
    def __init__(self, proj, state, k_prev, v_prev, u_prev, mixed_ref, ubuf):
        self.proj, self.state, self.k_prev, self.v_prev, self.u_prev = proj, state, k_prev, v_prev, u_prev
        self.mixed_ref, self.ubuf = mixed_ref, ubuf


def _mixer_math(chains, tabs, bias_variant, sdec, conv_w, sinks):
    dtab_ref, qtab_ref, ktab_ref, bias_ref = tabs
    t_rows = chains[0].mixed_ref.shape[0]
    pad = CHUNK - t_rows
    nt = (((1,), (1,)), ((), ()))
    tn = (((0,), (0,)), ((), ()))
    att_base = RET_HEADS * RET_DV
    conv_base = att_base + ATT_Q_HEADS * ATT_HEAD_DIM
    att_scale = ATT_HEAD_DIM ** -0.5
    kv_lanes = [slice(g * ATT_HEAD_DIM, (g + 1) * ATT_HEAD_DIM) for g in range(ATT_KV_HEADS)]
    heads, att_heads = range(RET_HEADS), range(ATT_Q_HEADS)

    def pad_rows(a):
        if pad == 0:
            return a
        return jnp.concatenate([a, jnp.zeros((pad, a.shape[1]), a.dtype)], axis=0)

    for ch in chains:
        proj = ch.proj
        ch.k_cur = pad_rows(proj(OFF_AK, OFF_AK + 128))
        ch.v_cur = pad_rows(proj(OFF_AV, OFF_AV + 128))
        k_all = jnp.concatenate([ch.k_prev, ch.k_cur], axis=0)
        ch.v_all = jnp.concatenate([ch.v_prev, ch.v_cur], axis=0)
        rq = [proj(OFF_RQ + h * RET_DK, OFF_RQ + (h + 1) * RET_DK) for h in heads]
        rk = [pad_rows(proj(OFF_RK + h * RET_DK, OFF_RK + (h + 1) * RET_DK)) for h in heads]
        ch.rv = [pad_rows(proj(OFF_RV + h * RET_DV, OFF_RV + (h + 1) * RET_DV)) for h in heads]
        ch.ret_s = [lax.dot_general(rq[h], rk[h], nt, preferred_element_type=F32) for h in heads]
        ch.inter = [jnp.dot(rq[h], ch.state[h].astype(BF16), preferred_element_type=F32) for h in heads]
        ch.att_s = []
        for hh in att_heads:
            q = proj(OFF_AQ + hh * ATT_HEAD_DIM, OFF_AQ + (hh + 1) * ATT_HEAD_DIM) * att_scale
            ch.att_s.append(lax.dot_general(q, k_all[:, kv_lanes[hh // ATT_GROUP]], nt, preferred_element_type=F32))
        ch.new_state = []
        for h in heads:
            ks = (rk[h].astype(F32) * ktab_ref[h]).astype(BF16)
            kv = lax.dot_general(ks, ch.rv[h], tn, preferred_element_type=F32)
            ch.new_state.append(sdec[h] * ch.state[h] + kv)

    for ch in chains:
        ch.ret_p = [(ch.ret_s[h] * dtab_ref[h]).astype(BF16) for h in heads]
        ch.att_e, ch.att_den = [], []
        for hh in att_heads:
            s = ch.att_s[hh] + bias_ref[bias_variant, hh]
            sink = sinks[hh]
            m = jnp.maximum(jnp.max(s, axis=-1, keepdims=True), sink)
            e = jnp.exp(s - m)
            ch.att_den.append(jnp.sum(e, axis=-1, keepdims=True) + jnp.exp(sink - m))
            ch.att_e.append(e.astype(BF16))
        proj, ubuf = ch.proj, ch.ubuf
        gate_b = proj(OFF_CB, OFF_CB + CONV_CH).astype(F32)
        u = proj(OFF_CC, OFF_CC + CONV_CH).astype(F32) * proj(OFF_CH, OFF_CH + CONV_CH).astype(F32)
        ubuf[6:8, :] = ch.u_prev
        ubuf[8:8 + t_rows, :] = u
        y = conv_w[0:1, :] * ubuf[6:6 + t_rows, :] + conv_w[1:2, :] * ubuf[7:7 + t_rows, :] + conv_w[2:3, :] * u
        ch.mixed_ref[:, conv_base:conv_base + CONV_CH] = (gate_b * y).astype(ch.mixed_ref.dtype)

    for ch in chains:
        ch.intra = [jnp.dot(ch.ret_p[h], ch.rv[h], preferred_element_type=F32) for h in heads]
        ch.att_o = [jnp.dot(ch.att_e[hh], ch.v_all[:, kv_lanes[hh // ATT_GROUP]], preferred_element_type=F32)
                    for hh in att_heads]

    for ch in chains:
        for h in heads:
            gate = ch.proj(OFF_RG + h * RET_DV, OFF_RG + (h + 1) * RET_DV).astype(F32)
            ry = ch.intra[h] + ch.inter[h] * qtab_ref[h]
            ryn = ry * lax.rsqrt(jnp.mean(ry * ry, axis=-1, keepdims=True) + EPS)
            ch.mixed_ref[:, h * RET_DV:(h + 1) * RET_DV] = (_silu(gate) * ryn).astype(ch.mixed_ref.dtype)
        for hh in att_heads:
            c0 = att_base + hh * ATT_HEAD_DIM
            ch.mixed_ref[:, c0:c0 + ATT_HEAD_DIM] = (ch.att_o[hh] / ch.att_den[hh]).astype(ch.mixed_ref.dtype)
    return [(ch.new_state, ch.k_cur, ch.v_cur) for ch in chains]


def _mix_prompt_kernel(sink_ref, *refs, layer, sdec, batch, n_carried):
    proj_refs, refs = refs[:batch], refs[batch:]
    (dtab_ref, qtab_ref, ktab_ref, bias_ref, convw_ref, wout_ref), refs = refs[:6], refs[6 + n_carried:]
    (mixed_ref, sret_ref, knew_ref, vnew_ref, cnew_ref,
     woutb_ref, s_scr, kprev_scr, vprev_scr, uprev_scr, ubuf) = refs
    c = pl.program_id(0)
    woutb_ref[0] = wout_ref[0].astype(BF16)

    @pl.when(c == 0)
    def _():
        s_scr[...] = jnp.zeros_like(s_scr)
        kprev_scr[...] = jnp.zeros_like(kprev_scr)
        vprev_scr[...] = jnp.zeros_like(vprev_scr)
        uprev_scr[...] = jnp.zeros_like(uprev_scr)

    sinks = [sink_ref[layer, hh] for hh in range(ATT_Q_HEADS)]
    bias_variant = jnp.where(c == 0, 0, 1)
    last = c == pl.num_programs(0) - 1
    chains = [_Chain(functools.partial(lambda ref, lo, hi: ref[:, lo:hi], proj_refs[b]),
                     [s_scr[b, h] for h in range(RET_HEADS)], kprev_scr[b], vprev_scr[b], uprev_scr[b],
                     mixed_ref.at[b], ubuf.at[b]) for b in range(batch)]
    results = [_mixer_math([ch], (dtab_ref, qtab_ref, ktab_ref, bias_ref), bias_variant, sdec, convw_ref[0], sinks)[0]
               for ch in chains]
    for b, (new_state, k_cur, v_cur) in enumerate(results):
        for h in range(RET_HEADS):
            s_scr[b, h] = new_state[h]
        kprev_scr[b] = k_cur
        vprev_scr[b] = v_cur
        uprev_scr[b] = ubuf[b, CHUNK + 6:CHUNK + 8, :]

        @pl.when(last)
        def _():
            for h in range(RET_HEADS):
                sret_ref[0, b, h] = new_state[h]
            knew_ref[0, b] = k_cur.astype(F32)
            vnew_ref[0, b] = v_cur.astype(F32)
            cnew_ref[0, b] = ubuf[b, CHUNK + 6:CHUNK + 8, :]


STATE_TAILS = [(RET_HEADS, RET_DK, RET_DV), (WINDOW, 128), (WINDOW, 128), (CONV_WIDTH - 1, CONV_CH)]


def _state_specs(layer, seqs_per_block, seq_block):
    return [pl.BlockSpec((1, seqs_per_block) + tail,
                         functools.partial(lambda z, i: (layer, seq_block(i)) + z, (0,) * len(tail)))
            for tail in STATE_TAILS]


def _state_shapes(n_seq):
    return [jax.ShapeDtypeStruct((DEPTH, n_seq) + tail, F32) for tail in STATE_TAILS]


def _carried_args(carried, n_fixed_inputs):
    if carried is None:
        return [], [], {}
    specs = [pl.BlockSpec(memory_space=pl.ANY)] * len(carried)
    return specs, list(carried), {n_fixed_inputs + k: 1 + k for k in range(len(carried))}


def _mix_prompt_call(layer, proj, batch, seq, conv_w, attn_sinks, tables, sdec, w_out, carried):
    n_chunks = seq // CHUNK
    wrows = D_MODEL // n_chunks
    assert D_MODEL % n_chunks == 0 and wrows % 16 == 0
    const3 = lambda ci: (0, 0, 0)
    const4 = lambda ci: (0, 0, 0, 0)
    proj_specs = [pl.BlockSpec((CHUNK, PROJ_WIDTH), functools.partial(lambda b, ci: (b * n_chunks + ci, 0), b))
                  for b in range(batch)]
    carried_specs, carried_args, aliases = _carried_args(carried, 1 + batch + 6)
    return pl.pallas_call(
        functools.partial(_mix_prompt_kernel, layer=layer, sdec=sdec, batch=batch, n_carried=len(carried_args)),
        grid=(n_chunks,),
        in_specs=[pl.BlockSpec(memory_space=pltpu.SMEM)] + proj_specs + [
            pl.BlockSpec((RET_HEADS, CHUNK, CHUNK), const3),
            pl.BlockSpec((RET_HEADS, CHUNK, CHUNK), const3),
            pl.BlockSpec((RET_HEADS, CHUNK, CHUNK), const3),
            pl.BlockSpec((2, ATT_Q_HEADS, CHUNK, 2 * CHUNK), const4),
            pl.BlockSpec((1, CONV_WIDTH, CONV_CH), lambda ci: (layer, 0, 0)),
            pl.BlockSpec((1, wrows, D_MODEL), lambda ci: (layer, ci, 0)),
        ] + carried_specs,
        out_specs=[pl.BlockSpec((batch, CHUNK, D_MODEL), lambda ci: (0, ci, 0))]
        + _state_specs(layer, batch, lambda ci: 0)
        + [pl.BlockSpec((1, wrows, D_MODEL), lambda ci: (0, ci, 0))],
        out_shape=[jax.ShapeDtypeStruct((batch, seq, D_MODEL), BF16)] + _state_shapes(batch)
        + [jax.ShapeDtypeStruct((1, D_MODEL, D_MODEL), BF16)],
        input_output_aliases=aliases,
        scratch_shapes=[
            pltpu.VMEM((batch, RET_HEADS, RET_DK, RET_DV), F32),
            pltpu.VMEM((batch, CHUNK, 128), BF16),
            pltpu.VMEM((batch, CHUNK, 128), BF16),
            pltpu.VMEM((batch, CONV_WIDTH - 1, CONV_CH), F32),
            pltpu.VMEM((batch, CHUNK + 8, CONV_CH), F32),
        ],
        compiler_params=_params("arbitrary"),
        name="mix_prompt",
    )(attn_sinks, *([proj] * batch), *tables, conv_w, w_out, *carried_args)


def _mix_sample_kernel(sink_ref, proj_ref, dtab_ref, qtab_ref, ktab_ref, bias_ref, convw_ref,
                       sin_ref, kbuf_ref, vbuf_ref, cbuf_ref, *refs, layer, sdec, t_valid, seqs_per_step, n_carried):
    assert 2 * SEQ_PAD == MIX_ROWS and seqs_per_step % 2 == 0
    mixed_ref, sret_ref, knew_ref, vnew_ref, cnew_ref, ubuf, kvbuf, mix_scr = refs[n_carried:]
    sin_ref, kbuf_ref, vbuf_ref, cbuf_ref = sin_ref.at[0], kbuf_ref.at[0], vbuf_ref.at[0], cbuf_ref.at[0]
    sret_ref, knew_ref, vnew_ref, cnew_ref = sret_ref.at[0], knew_ref.at[0], vnew_ref.at[0], cnew_ref.at[0]

    def seq_proj(s, lo, hi):
        pair = proj_ref[(s // 2) * MIX_ROWS:(s // 2 + 1) * MIX_ROWS, lo:hi].astype(F32)
        own = pair[(s % 2) * SEQ_PAD:(s % 2 + 1) * SEQ_PAD]
        return jnp.concatenate([own, jnp.zeros((MIX_ROWS - SEQ_PAD, hi - lo), F32)], axis=0).astype(BF16)

    sinks = [sink_ref[layer, hh] for hh in range(ATT_Q_HEADS)]
    chains = [_Chain(functools.partial(seq_proj, s), [sin_ref[s, h] for h in range(RET_HEADS)],
                     kbuf_ref[s].astype(BF16), vbuf_ref[s].astype(BF16), cbuf_ref[s], mix_scr.at[s], ubuf.at[s])
              for s in range(seqs_per_step)]
    results = _mixer_math(chains, (dtab_ref, qtab_ref, ktab_ref, bias_ref), 0, sdec, convw_ref[0], sinks)
    for s, (new_state, _, _) in enumerate(results):
        for h in range(RET_HEADS):
            sret_ref[s, h] = new_state[h]
        for j, (src_ref, off, dst_ref) in enumerate(((kbuf_ref, OFF_AK, knew_ref), (vbuf_ref, OFF_AV, vnew_ref))):
            kvbuf[s, j, 0:WINDOW, :] = src_ref[s]
            kvbuf[s, j, WINDOW:WINDOW + MIX_ROWS, :] = seq_proj(s, off, off + 128).astype(F32)
            dst_ref[s] = kvbuf[s, j, t_valid:t_valid + WINDOW, :]
        cnew_ref[s] = ubuf[s, 8 + t_valid - 2:8 + t_valid, :]
    for p in range(seqs_per_step // 2):
        pair = jnp.concatenate([mix_scr[2 * p, 0:SEQ_PAD, :], mix_scr[2 * p + 1, 0:SEQ_PAD, :]], axis=0)
        mixed_ref[p * MIX_ROWS:(p + 1) * MIX_ROWS, :] = pair.astype(mixed_ref.dtype)


def _mix_sample_call(layer, proj, row0, conv_w, attn_sinks, tables, sdec, states_in, t_valid, carried):
    nb = states_in[0].shape[1]
    sps = 8
    assert nb % sps == 0 and row0 % (sps * SEQ_PAD) == 0
    blk0 = row0 // (sps * SEQ_PAD)
    const3 = lambda bi: (0, 0, 0)
    const4 = lambda bi: (0, 0, 0, 0)
    carried_specs, carried_args, aliases = _carried_args(carried, 11)
    return pl.pallas_call(
        functools.partial(_mix_sample_kernel, layer=layer, sdec=sdec, t_valid=t_valid, seqs_per_step=sps,
                          n_carried=len(carried_args)),
        grid=(nb // sps,),
        in_specs=[
            pl.BlockSpec(memory_space=pltpu.SMEM),
            pl.BlockSpec((sps * SEQ_PAD, PROJ_WIDTH), lambda bi: (blk0 + bi, 0)),
            pl.BlockSpec((RET_HEADS, MIX_ROWS, CHUNK), const3),
            pl.BlockSpec((RET_HEADS, MIX_ROWS, CHUNK), const3),
            pl.BlockSpec((RET_HEADS, CHUNK, CHUNK), const3),
            pl.BlockSpec((1, ATT_Q_HEADS, MIX_ROWS, 2 * CHUNK), const4),
            pl.BlockSpec((1, CONV_WIDTH, CONV_CH), lambda bi: (layer, 0, 0)),
        ] + _state_specs(layer, sps, lambda bi: bi) + carried_specs,
        out_specs=[pl.BlockSpec((sps * SEQ_PAD, D_MODEL), lambda bi: (bi, 0))]
        + _state_specs(layer, sps, lambda bi: bi),
        out_shape=[jax.ShapeDtypeStruct((nb * SEQ_PAD, D_MODEL), BF16)] + _state_shapes(nb),
        input_output_aliases=aliases,
        scratch_shapes=[
            pltpu.VMEM((sps, MIX_ROWS + 8, CONV_CH), F32),
            pltpu.VMEM((sps, 2, WINDOW + MIX_ROWS, 128), F32),
            pltpu.VMEM((sps, MIX_ROWS, D_MODEL), F32),
        ],
        compiler_params=_params("arbitrary"),
        name="mix_sample",
    )(attn_sinks, proj, *tables, conv_w, *states_in, *carried_args)


def kernel(x_prompt, x_sample, state_ret, cache_win_k, cache_win_v, state_conv, c_prompt, c_sample,
           w_in, w_out, conv_w, attn_sinks, norm_g, w_ada, b_ada, w_ff_gate, w_ff_up, w_ff_down):
    batch, seq, _ = x_prompt.shape
    nb, t_valid, _ = x_sample.shape
    w_buf = cache_win_k.shape[2]
    assert w_buf == WINDOW and seq % CHUNK == 0 and t_valid <= SEQ_PAD and nb + batch <= MOD_ROWS
    rows_p, rows_s = batch * seq, nb * SEQ_PAD
    rows = rows_p + rows_s

    c_all = jnp.concatenate([c_sample, c_prompt, jnp.zeros((MOD_ROWS - nb - batch, D_MODEL), F32)], axis=0)
    mod4 = _ada_call(c_all, w_ada, b_ada).reshape(DEPTH, MOD_ROWS, 1, N_MOD * D_MODEL)
    gains = norm_g.reshape(DEPTH * 4, 1, D_MODEL)

    tiles_out = tiles_down = _RowTiles(256, rows_p, rows_s, seq, nb)
    tm_inproj, tm_gateup = rows // 6, rows // 8
    assert rows % 24 == 0 and tm_inproj % 16 == 0 and tm_gateup % 16 == 0
    tabs_p, sdec_p = _mixer_tables(CHUNK, CHUNK, True)
    tabs_s, sdec_s = _mixer_tables(MIX_ROWS, t_valid, False)

    xp = x_prompt.reshape(rows_p, D_MODEL)
    xs = jnp.pad(x_sample, ((0, 0), (0, SEQ_PAD - t_valid), (0, 0))).reshape(rows_s, D_MODEL)
    states_in = (state_ret, cache_win_k.reshape(DEPTH, nb, w_buf, 128), cache_win_v.reshape(DEPTH, nb, w_buf, 128),
                 state_conv)

    h = _prenorm_call(tiles_out, 0, xp, xs, gains, mod4)
    st_p = st_s = None
    for l in range(DEPTH):
        proj = _inproj_call(l, h, w_in, tm_inproj, 1280)
        mixed_p, *st_p, w_out_b = _mix_prompt_call(l, proj, batch, seq, conv_w, attn_sinks, tabs_p, sdec_p, w_out,
                                                   st_p)
        mixed_s, *st_s = _mix_sample_call(l, proj, rows_p, conv_w, attn_sinks, tabs_s, sdec_s, states_in, t_valid,
                                          st_s)
        mixed = (mixed_p.reshape(rows_p, D_MODEL), mixed_s)
        xp, xs, h2 = _resid_call("out_proj", tiles_out, 128, mixed, w_out_b, xp, xs, gains, mod4, l,
                                 1, MOD_G1, (l, 2, MOD_SC2, MOD_SH2))
        hid, w_down_b = _gateup_call(l, h2, w_ff_gate, w_ff_up, w_ff_down, tm_gateup, 512)
        next_h = (l + 1, 0, MOD_SC1, MOD_SH1) if l + 1 < DEPTH else None
        res = _resid_call("ffn_down", tiles_down, 128, hid, w_down_b, xp, xs, gains, mod4, l,
                          3, MOD_G2, next_h)
        xp, xs = res[0], res[1]
        if next_h is not None:
            h = res[2]

    kv_shape_p = (DEPTH, batch, w_buf, ATT_KV_HEADS, ATT_HEAD_DIM)
    kv_shape_s = (DEPTH, nb, w_buf, ATT_KV_HEADS, ATT_HEAD_DIM)
    return (
        xp.reshape(batch, seq, D_MODEL),
        xs.reshape(nb, SEQ_PAD, D_MODEL)[:, :t_valid],
        st_p[0], st_s[0],
        st_p[1].reshape(kv_shape_p), st_s[1].reshape(kv_shape_s),
        st_p[2].reshape(kv_shape_p), st_s[2].reshape(kv_shape_s),
        st_p[3], st_s[3],
    )
```
